```python
import math
import jax, jax.numpy as jnp
from jax import lax
import numpy as np

D_MODEL = 2048
BATCH = 4
SEQ = 4096
DEPTH = 2
DEC_BATCH = 32
DEC_SEQ = 16
PAST_LEN = 4096

CHUNK = 64
N_A_LAYERS = DEPTH // 2
N_B_LAYERS = DEPTH - N_A_LAYERS
N_DENSE_LAYERS = (DEPTH + 1) // 2
N_MOE_LAYERS = DEPTH // 2

SSM_EXPAND = 2
D_INNER = SSM_EXPAND * D_MODEL
SSM_HEAD_DIM = 64
SSM_HEADS = D_INNER // SSM_HEAD_DIM
SSM_GROUPS = 8
SSM_HEADS_PER_GROUP = SSM_HEADS // SSM_GROUPS
SSM_STATE = 128
CONV_WIDTH = 4
CONV_DIM = D_INNER + 2 * SSM_GROUPS * SSM_STATE
IN_PROJ_DIM = D_INNER + CONV_DIM + SSM_HEADS
SSD_CHUNK = CHUNK

N_Q_HEADS = 32
N_KV_HEADS = 4
Q_PER_KV = N_Q_HEADS // N_KV_HEADS
HEAD_DIM = 64
WINDOW = 128
N_BAND_CHUNKS = WINDOW // CHUNK + 1
BAND = N_BAND_CHUNKS * CHUNK
ROPE_THETA = 10000.0

D_FF = 5632
N_EXPERTS = 8
TOP_K = 2
EXPERT_FF = 7168

PLE_DIM = 256
EPS = 1e-6

kernel_name = 'yoco_ssd_swa_sink_stream_step'


def _rmsnorm(x, g):
    xf = x.astype(jnp.float32)
    inv = lax.rsqrt(jnp.mean(xf * xf, axis=-1, keepdims=True) + EPS)
    return (xf * inv * g.astype(jnp.float32)).astype(x.dtype)


def _rope(x, pos):
    half = HEAD_DIM // 2
    inv_freq = ROPE_THETA ** (-jnp.arange(half, dtype=jnp.float32) * 2.0 / HEAD_DIM)
    ang = pos.astype(jnp.float32)[:, None] * inv_freq[None, :]
    cos = jnp.cos(ang)[None, :, None, :]
    sin = jnp.sin(ang)[None, :, None, :]
    xf = x.astype(jnp.float32)
    x1, x2 = xf[..., :half], xf[..., half:]
    return jnp.concatenate([x1 * cos - x2 * sin, x2 * cos + x1 * sin], axis=-1).astype(x.dtype)


def _sink_softmax(s, sink):
    m = jnp.maximum(jnp.max(s, axis=-1, keepdims=True), sink)
    e = jnp.exp(s - m)
    return e / (jnp.sum(e, axis=-1, keepdims=True) + jnp.exp(sink - m))


def _ssd(X, A, B, C, h0):
    b, L = X.shape[0], X.shape[1]
    q = min(SSD_CHUNK, L)
    nc = L // q
    X = X.reshape(b, nc, q, SSM_GROUPS, SSM_HEADS_PER_GROUP, SSM_HEAD_DIM)
    A = A.reshape(b, nc, q, SSM_GROUPS, SSM_HEADS_PER_GROUP)
    B = B.reshape(b, nc, q, SSM_GROUPS, SSM_STATE)
    C = C.reshape(b, nc, q, SSM_GROUPS, SSM_STATE)
    a_cs = jnp.cumsum(A, axis=2)
    causal = jnp.tril(jnp.ones((q, q), dtype=bool))[None, None, :, :, None, None]
    seg = a_cs[:, :, :, None] - a_cs[:, :, None, :]
    decay_ls = jnp.exp(jnp.where(causal, seg, -jnp.inf))
    cb = jnp.einsum('bclgn,bcsgn->bclsg', C, B)
    y_diag = jnp.einsum('bclsge,bcsgep->bclgep', cb[..., None] * decay_ls, X)
    decay_to_end = jnp.exp(a_cs[:, :, -1:] - a_cs)
    chunk_states = jnp.einsum('bclgn,bclge,bclgep->bcgepn', B, decay_to_end, X)
    chunk_decay = jnp.exp(a_cs[:, :, -1])

    def step(h, inp):
        st, dec = inp
        return h * dec[..., None, None] + st, h

    h_last, h_in = lax.scan(step, h0, (jnp.moveaxis(chunk_states, 1, 0), jnp.moveaxis(chunk_decay, 1, 0)))
    h_in = jnp.moveaxis(h_in, 0, 1)
    y_off = jnp.einsum('bclgn,bcgepn,bclge->bclgep', C, h_in, jnp.exp(a_cs))
    y = (y_diag + y_off).reshape(b, L, SSM_GROUPS, SSM_HEADS_PER_GROUP, SSM_HEAD_DIM)
    return y, h_last


def _mamba2(h, conv_state, ssm_state, in_proj, conv_w, conv_b, dt_bias, a_log, d_skip, norm_w, out_proj):
    b, L, _ = h.shape
    zxbcdt = h @ in_proj
    z = zxbcdt[..., :D_INNER]
    xbc = zxbcdt[..., D_INNER:D_INNER + CONV_DIM]
    dt = zxbcdt[..., D_INNER + CONV_DIM:]
    xpad = jnp.concatenate([conv_state.astype(xbc.dtype), xbc], axis=1)
    conv = conv_b
    for t in range(CONV_WIDTH):
        conv = conv + xpad[:, t:t + L] * conv_w[t]
    new_conv = xpad[:, L:]
    xbc = jax.nn.silu(conv)
    f32 = jnp.float32
    xs = xbc[..., :D_INNER].reshape(b, L, SSM_GROUPS, SSM_HEADS_PER_GROUP, SSM_HEAD_DIM).astype(f32)
    Bm = xbc[..., D_INNER:D_INNER + SSM_GROUPS * SSM_STATE].reshape(b, L, SSM_GROUPS, SSM_STATE).astype(f32)
    Cm = xbc[..., D_INNER + SSM_GROUPS * SSM_STATE:].reshape(b, L, SSM_GROUPS, SSM_STATE).astype(f32)
    dt = jax.nn.softplus(dt.astype(f32) + dt_bias.astype(f32)).reshape(b, L, SSM_GROUPS, SSM_HEADS_PER_GROUP)
    A = -jnp.exp(a_log.astype(f32)).reshape(SSM_GROUPS, SSM_HEADS_PER_GROUP)
    h0 = ssm_state.astype(f32).reshape(b, SSM_GROUPS, SSM_HEADS_PER_GROUP, SSM_HEAD_DIM, SSM_STATE)
    y, h_last = _ssd(xs * dt[..., None], dt * A, Bm, Cm, h0)
    y = y + xs * d_skip.astype(f32).reshape(SSM_GROUPS, SSM_HEADS_PER_GROUP)[..., None]
    y = y.reshape(b, L, D_INNER).astype(h.dtype)
    y = _rmsnorm(y * jax.nn.silu(z), norm_w)
    new_ssm = h_last.reshape(b, SSM_HEADS, SSM_HEAD_DIM, SSM_STATE).astype(ssm_state.dtype)
    return y @ out_proj, new_conv, new_ssm


def _banded_sink_attention(q, k, v, sink):
    b, S = q.shape[0], q.shape[1]
    nc = S // CHUNK
    qb = q.reshape(b, nc, CHUNK, N_KV_HEADS, Q_PER_KV, HEAD_DIM)
    padw = ((0, 0), (WINDOW, 0), (0, 0), (0, 0))
    kp, vp = jnp.pad(k, padw), jnp.pad(v, padw)
    kb = jnp.concatenate([kp[:, o * CHUNK:o * CHUNK + S].reshape(b, nc, CHUNK, N_KV_HEADS, HEAD_DIM)
                          for o in range(N_BAND_CHUNKS)], axis=2)
    vb = jnp.concatenate([vp[:, o * CHUNK:o * CHUNK + S].reshape(b, nc, CHUNK, N_KV_HEADS, HEAD_DIM)
                          for o in range(N_BAND_CHUNKS)], axis=2)
    key_pos = jnp.arange(nc)[:, None] * CHUNK - WINDOW + jnp.arange(BAND)[None, :]
    valid = (key_pos >= 0)[None, :, None, None, None, :]
    s = jnp.einsum('bcqkgd,bcjkd->bckgqj', qb, kb, preferred_element_type=jnp.float32)
    s = jnp.where(valid, s, -jnp.inf)
    p = _sink_softmax(s, sink.astype(jnp.float32).reshape(N_KV_HEADS, Q_PER_KV)[None, None, :, :, None, None])
    o = jnp.einsum('bckgqj,bcjkd->bcqkgd', p.astype(v.dtype), vb)
    return o.reshape(b, S, N_Q_HEADS * HEAD_DIM)


def _cached_sink_attention(q, k_all, v_all, sink):
    b, T = q.shape[0], q.shape[1]
    qg = q.reshape(b, T, N_KV_HEADS, Q_PER_KV, HEAD_DIM)
    s = jnp.einsum('btkgd,bjkd->bkgtj', qg, k_all, preferred_element_type=jnp.float32)
    p = _sink_softmax(s, sink.astype(jnp.float32).reshape(N_KV_HEADS, Q_PER_KV)[None, :, :, None, None])
    o = jnp.einsum('bkgtj,bjkd->btkgd', p.astype(v_all.dtype), v_all)
    return o.reshape(b, T, N_Q_HEADS * HEAD_DIM)


def _swiglu(h, w_gate_up, w_down, ff):
    gu = h @ w_gate_up
    return (jax.nn.silu(gu[..., :ff]) * gu[..., ff:]) @ w_down


def _moe(h, router, w_gate_up, w_down):
    b, L, _ = h.shape
    hf = h.reshape(b * L, D_MODEL)
    logits = (hf @ router).astype(jnp.float32)
    vals, idx = lax.top_k(logits, TOP_K)
    w = jax.nn.softmax(vals, axis=-1)
    gates = jnp.einsum('tk,tke->te', w, jax.nn.one_hot(idx, N_EXPERTS, dtype=jnp.float32)).astype(h.dtype)
    out = jnp.zeros_like(hf)
    for e in range(N_EXPERTS):
        out = out + gates[:, e:e + 1] * _swiglu(hf, w_gate_up[e], w_down[e], EXPERT_FF)
    return out.reshape(b, L, D_MODEL)


def _trunk(x, p, pos_offset, conv_states, ssm_states, cache_k, cache_v, prm):
    b, L, _ = x.shape
    pos = pos_offset + jnp.arange(L, dtype=jnp.int32)
    is_prompt = cache_k is None
    new_conv, new_ssm = [], []
    k_sh = v_sh = None
    for i in range(DEPTH):
        h = _rmsnorm(x, prm['norm_mix'][i])
        if i < N_A_LAYERS:
            y, cs, ss = _mamba2(h, conv_states[i], ssm_states[i], prm['mamba_in_proj'][i], prm['mamba_conv_w'][i],
                                prm['mamba_conv_b'][i], prm['mamba_dt_bias'][i], prm['mamba_a_log'][i],
                                prm['mamba_d'][i], prm['mamba_norm'][i], prm['mamba_out_proj'][i])
            new_conv.append(cs)
            new_ssm.append(ss)
        else:
            if i == N_A_LAYERS:
                kv = (_rmsnorm(x, prm['norm_kv']) @ prm['w_kv']).reshape(b, L, 2, N_KV_HEADS, HEAD_DIM)
                k_sh = _rope(kv[:, :, 0], pos)
                v_sh = kv[:, :, 1]
                if not is_prompt:
                    k_sh = jnp.concatenate([cache_k.astype(k_sh.dtype), k_sh], axis=1)
                    v_sh = jnp.concatenate([cache_v.astype(v_sh.dtype), v_sh], axis=1)
            j = i - N_A_LAYERS
            q = _rope((h @ prm['w_q'][j]).reshape(b, L, N_Q_HEADS, HEAD_DIM), pos) * (HEAD_DIM ** -0.5)
            if is_prompt:
                o = _banded_sink_attention(q, k_sh, v_sh, prm['attn_sinks'][j])
            else:
                o = _cached_sink_attention(q, k_sh, v_sh, prm['attn_sinks'][j])
            y = o @ prm['w_o'][j]
        x = x + y
        h = _rmsnorm(x, prm['norm_ffn'][i])
        if i % 2 == 0:
            x = x + _swiglu(h, prm['ffn_w_gate_up'][i // 2], prm['ffn_w_down'][i // 2], D_FF)
        else:
            x = x + _moe(h, prm['moe_router'][i // 2], prm['moe_w_gate_up'][i // 2], prm['moe_w_down'][i // 2])
        gate = jax.nn.sigmoid(_rmsnorm(x, prm['norm_ple'][i]) @ prm['ple_gate'][i])
        x = x + gate * (p[i] @ prm['ple_proj'][i])
    out = _rmsnorm(x, prm['norm_final'])
    new_k = k_sh[:, -WINDOW:]
    new_v = v_sh[:, -WINDOW:]
    return out, jnp.stack(new_conv), jnp.stack(new_ssm), new_k, new_v


def setup_inputs(seed: int = 0) -> dict:
    key = jax.random.key(seed)
    ks = iter(jax.random.split(key, 40))
    f32 = jnp.float32

    def nrm(shape, scale):
        return jax.random.normal(next(ks), shape, f32) * scale

    def gain(shape):
        return 1.0 + nrm(shape, 0.02)

    dt0 = jnp.exp(jax.random.uniform(next(ks), (N_A_LAYERS, SSM_HEADS), f32,
                                     minval=math.log(1e-3), maxval=math.log(1e-1)))
    return {
        'x_prompt': nrm((BATCH, SEQ, D_MODEL), 1.0),
        'x_sample': nrm((DEC_BATCH, DEC_SEQ, D_MODEL), 1.0),
        'p_prompt': nrm((DEPTH, BATCH, SEQ, PLE_DIM), 1.0),
        'p_sample': nrm((DEPTH, DEC_BATCH, DEC_SEQ, PLE_DIM), 1.0),
        'cache_conv': nrm((N_A_LAYERS, DEC_BATCH, CONV_WIDTH - 1, CONV_DIM), 1.0),
        'state_ssm': nrm((N_A_LAYERS, DEC_BATCH, SSM_HEADS, SSM_HEAD_DIM, SSM_STATE), 0.1),
        'cache_k': nrm((DEC_BATCH, WINDOW, N_KV_HEADS, HEAD_DIM), 1.0),
        'cache_v': nrm((DEC_BATCH, WINDOW, N_KV_HEADS, HEAD_DIM), 1.0),
        'norm_mix': gain((DEPTH, D_MODEL)),
        'norm_ffn': gain((DEPTH, D_MODEL)),
        'norm_ple': gain((DEPTH, D_MODEL)),
        'norm_final': gain((D_MODEL,)),
        'mamba_in_proj': nrm((N_A_LAYERS, D_MODEL, IN_PROJ_DIM), D_MODEL ** -0.5),
        'mamba_conv_w': nrm((N_A_LAYERS, CONV_WIDTH, CONV_DIM), CONV_WIDTH ** -0.5),
        'mamba_conv_b': nrm((N_A_LAYERS, CONV_DIM), 0.02),
        'mamba_dt_bias': dt0 + jnp.log(-jnp.expm1(-dt0)),
        'mamba_a_log': jnp.log(jax.random.uniform(next(ks), (N_A_LAYERS, SSM_HEADS), f32, minval=1.0, maxval=16.0)),
        'mamba_d': 1.0 + nrm((N_A_LAYERS, SSM_HEADS), 0.1),
        'mamba_norm': gain((N_A_LAYERS, D_INNER)),
        'mamba_out_proj': nrm((N_A_LAYERS, D_INNER, D_MODEL), D_INNER ** -0.5),
        'norm_kv': gain((D_MODEL,)),
        'w_kv': nrm((D_MODEL, 2 * N_KV_HEADS * HEAD_DIM), D_MODEL ** -0.5),
        'w_q': nrm((N_B_LAYERS, D_MODEL, N_Q_HEADS * HEAD_DIM), D_MODEL ** -0.5),
        'attn_sinks': nrm((N_B_LAYERS, N_Q_HEADS), 1.0),
        'w_o': nrm((N_B_LAYERS, N_Q_HEADS * HEAD_DIM, D_MODEL), (N_Q_HEADS * HEAD_DIM) ** -0.5),
        'ffn_w_gate_up': nrm((N_DENSE_LAYERS, D_MODEL, 2 * D_FF), D_MODEL ** -0.5),
        'ffn_w_down': nrm((N_DENSE_LAYERS, D_FF, D_MODEL), D_FF ** -0.5),
        'moe_router': nrm((N_MOE_LAYERS, D_MODEL, N_EXPERTS), D_MODEL ** -0.5),
        'moe_w_gate_up': nrm((N_MOE_LAYERS, N_EXPERTS, D_MODEL, 2 * EXPERT_FF), D_MODEL ** -0.5),
        'moe_w_down': nrm((N_MOE_LAYERS, N_EXPERTS, EXPERT_FF, D_MODEL), EXPERT_FF ** -0.5),
        'ple_proj': nrm((DEPTH, PLE_DIM, D_MODEL), PLE_DIM ** -0.5),
        'ple_gate': nrm((DEPTH, D_MODEL, D_MODEL), D_MODEL ** -0.5),
    }


def reference(x_prompt, x_sample, p_prompt, p_sample, cache_conv, state_ssm, cache_k, cache_v,
              norm_mix, norm_ffn, norm_ple, norm_final, mamba_in_proj, mamba_conv_w, mamba_conv_b,
              mamba_dt_bias, mamba_a_log, mamba_d, mamba_norm, mamba_out_proj, norm_kv, w_kv, w_q,
              attn_sinks, w_o, ffn_w_gate_up, ffn_w_down, moe_router, moe_w_gate_up, moe_w_down,
              ple_proj, ple_gate):
    prm = dict(norm_mix=norm_mix, norm_ffn=norm_ffn, norm_ple=norm_ple, norm_final=norm_final,
               mamba_in_proj=mamba_in_proj, mamba_conv_w=mamba_conv_w, mamba_conv_b=mamba_conv_b,
               mamba_dt_bias=mamba_dt_bias, mamba_a_log=mamba_a_log, mamba_d=mamba_d,
               mamba_norm=mamba_norm, mamba_out_proj=mamba_out_proj, norm_kv=norm_kv, w_kv=w_kv,
               w_q=w_q, attn_sinks=attn_sinks, w_o=w_o, ffn_w_gate_up=ffn_w_gate_up,
               ffn_w_down=ffn_w_down, moe_router=moe_router, moe_w_gate_up=moe_w_gate_up,
               moe_w_down=moe_w_down, ple_proj=ple_proj, ple_gate=ple_gate)
    bp = x_prompt.shape[0]
    conv0 = jnp.zeros((N_A_LAYERS, bp, CONV_WIDTH - 1, CONV_DIM), x_prompt.dtype)
    ssm0 = jnp.zeros((N_A_LAYERS, bp, SSM_HEADS, SSM_HEAD_DIM, SSM_STATE), x_prompt.dtype)
    y_prompt, conv_prompt, ssm_prompt, k_prompt, v_prompt = _trunk(
        x_prompt, p_prompt, 0, conv0, ssm0, None, None, prm)
    y_sample, conv_sample, ssm_sample, k_sample, v_sample = _trunk(
        x_sample, p_sample, PAST_LEN, cache_conv, state_ssm, cache_k, cache_v, prm)
    return (y_prompt, y_sample, conv_prompt, ssm_prompt, k_prompt, v_prompt,
            conv_sample, ssm_sample, k_sample, v_sample)
```

```python
import functools

import jax
import jax.numpy as jnp
from jax import lax
from jax.experimental import pallas as pl
from jax.experimental.pallas import tpu as pltpu

F32 = jnp.float32
BF16 = jnp.bfloat16
I32 = jnp.int32
HI = lax.Precision.HIGHEST

D_MODEL = 2048
D_INNER = 4096
SSM_HEADS = 64
SSM_HEAD_DIM = 64
SSM_GROUPS = 8
SSM_STATE = 128
CONV_WIDTH = 4
CONV_DIM = D_INNER + 2 * SSM_GROUPS * SSM_STATE
N_Q_HEADS = 32
N_KV_HEADS = 4
Q_PER_KV = N_Q_HEADS // N_KV_HEADS
HEAD_DIM = 64
WINDOW = 128
CHUNK = 64
ROPE_THETA = 10000.0
D_FF = 5632
N_EXPERTS = 8
EXPERT_FF = 7168
PLE_DIM = 256
PAST_LEN = 4096
EPS = 1e-6

LANES = 128
MIB = 1024 * 1024
NT_DIMS = (((1,), (1,)), ((), ()))


def _params(n_axes, vmem_mib):
    return pltpu.CompilerParams(dimension_semantics=("arbitrary",) * n_axes,
                                vmem_limit_bytes=vmem_mib * MIB)


def _pick(n, pref):
    for t in (1536, 1024, 768, 512, 384, 256, 128, 96, 64, 32, 16, 8):
        if t <= pref and n % t == 0:
            return t
    raise ValueError(f"no tile for {n}")


def _dot(a, b):
    return jnp.dot(a, b, preferred_element_type=F32)


def _dot_nt(a, b, precision=None):
    return lax.dot_general(a, b, NT_DIMS, precision=precision, preferred_element_type=F32)


def _sigmoid(x):
    return 1.0 / (1.0 + jnp.exp(-x))


def _softplus(x):
    return jnp.maximum(x, 0.0) + jnp.log1p(jnp.exp(-jnp.abs(x)))


def _eye(n, dtype):
    r = lax.broadcasted_iota(I32, (n, n), 0)
    c = lax.broadcasted_iota(I32, (n, n), 1)
    return (r == c).astype(dtype)


def _rms_rows(x, g):
    inv = lax.rsqrt(jnp.mean(x * x, axis=-1, keepdims=True) + EPS)
    return x * inv * g


def _rms_body(x_ref, g_ref, o_ref):
    o_ref[...] = _rms_rows(x_ref[...], g_ref[...]).astype(o_ref.dtype)


def _rmsnorm(x, g, out_dtype):
    T, D = x.shape
    tr = _pick(T, 512)
    return pl.pallas_call(
        _rms_body, out_shape=jax.ShapeDtypeStruct((T, D), out_dtype), grid=(T // tr,),
        in_specs=[pl.BlockSpec((tr, D), lambda i: (i, 0)), pl.BlockSpec((1, D), lambda i: (0, 0))],
        out_specs=pl.BlockSpec((tr, D), lambda i: (i, 0)),
        compiler_params=_params(1, 32), name="rmsnorm")(x, g.reshape(1, D))


def _mm_plain_body(x_ref, w_ref, o_ref):
    o_ref[...] = _dot(x_ref[...], w_ref[...]).astype(o_ref.dtype)


def _mm_res_body(x_ref, w_ref, r_ref, o_ref):
    o_ref[...] = r_ref[...] + _dot(x_ref[...], w_ref[...])


def _mm_swiglu_body(x_ref, wg_ref, wu_ref, o_ref):
    x = x_ref[...]
    g = _dot(x, wg_ref[...])
    o_ref[...] = (g * _sigmoid(g) * _dot(x, wu_ref[...])).astype(o_ref.dtype)


def _mm_ple_body(h_ref, wg_ref, p_ref, wp_ref, r_ref, o_ref):
    gate = _sigmoid(_dot(h_ref[...], wg_ref[...]))
    o_ref[...] = r_ref[...] + gate * _dot(p_ref[...], wp_ref[...])


def _mm_rope_body(x_ref, w_ref, cos_ref, sin_ref, o_ref, *, n_rope, scale):
    acc = _dot(x_ref[...], w_ref[...])
    tm, tn = acc.shape
    cos = cos_ref[...]
    sin = sin_ref[...]
    lane = lax.broadcasted_iota(I32, (tm, LANES), 1)
    first_half = (lane % HEAD_DIM) < (HEAD_DIM // 2)
    for c in range(tn // LANES):
        x = acc[:, c * LANES:(c + 1) * LANES]
        if c < n_rope:
            partner = jnp.where(first_half, pltpu.roll(x, LANES - HEAD_DIM // 2, 1),
                                pltpu.roll(x, HEAD_DIM // 2, 1))
            x = (x * cos + partner * sin) * scale
        o_ref[:, c * LANES:(c + 1) * LANES] = x.astype(o_ref.dtype)


def _mm_call(body, T, N, tm, tn, operands, out_dtype, vmem_mib, name):
    in_specs, args = [], []
    for op in operands:
        kind, a = op[0], op[1]
        if kind == "lhs":
            in_specs.append(pl.BlockSpec((tm, a.shape[1]), lambda j, i: (i, 0)))
        elif kind == "w":
            in_specs.append(pl.BlockSpec((a.shape[0], tn), functools.partial(lambda j, i, off: (0, j + off), off=op[2])))
        elif op[2]:
            in_specs.append(pl.BlockSpec((tm, tn), lambda j, i: (i, j)))
        else:
            in_specs.append(pl.BlockSpec((tm, a.shape[1]), lambda j, i: (i, 0)))
        args.append(a)
    return pl.pallas_call(
        body, out_shape=jax.ShapeDtypeStruct((T, N), out_dtype), grid=(N // tn, T // tm),
        in_specs=in_specs, out_specs=pl.BlockSpec((tm, tn), lambda j, i: (i, j)),
        compiler_params=_params(2, vmem_mib), name=name)(*args)


def _ssd_body(*refs, Q, nc, has_init):
    if has_init:
        (z_ref, xbc_ref, dt_ref, cw_ref, cb_ref, dtb_ref, alog_ref, dfull_ref, nw_ref, conv0_ref, ssm0_ref,
         _, g_ref, convo_ref, ssmo_ref, xp, HT, ysc, act, xsb) = refs
    else:
        (z_ref, xbc_ref, dt_ref, cw_ref, cb_ref, dtb_ref, alog_ref, dfull_ref, nw_ref,
         g_ref, convo_ref, ssmo_ref, xp, HT, ysc, act, xsb) = refs
    c = pl.program_id(1)
    HPG = SSM_HEADS // SSM_GROUPS
    eye_h = _eye(SSM_HEADS, F32)
    eye_n = _eye(SSM_STATE, F32)
    eye_n_bf = _eye(SSM_STATE, BF16)

    @pl.when(c == 0)
    def _init():
        xp[0:8, :] = jnp.zeros((8, CONV_DIM), F32)
        if has_init:
            xp[8 - (CONV_WIDTH - 1):8, :] = conv0_ref[...]
            for h in range(SSM_HEADS):
                g, e = divmod(h, HPG)
                HT[g, :, e * SSM_HEAD_DIM:(e + 1) * SSM_HEAD_DIM] = _dot_nt(eye_n, ssm0_ref[h], HI)
        else:
            HT[...] = jnp.zeros(HT.shape, F32)

    xc = xbc_ref[...]
    xp[8:8 + Q, :] = xc
    cw = cw_ref[...]
    conv = cb_ref[...]
    for t in range(CONV_WIDTH - 1):
        off = 8 - (CONV_WIDTH - 1) + t
        conv = conv + xp[off:off + Q, :] * cw[t:t + 1, :]
    conv = conv + xc * cw[CONV_WIDTH - 1:CONV_WIDTH, :]
    act[...] = conv * _sigmoid(conv)

    @pl.when(c == nc - 1)
    def _conv_out():
        convo_ref[...] = xp[8 + Q - (CONV_WIDTH - 1):8 + Q, :]

    xp[0:8, :] = xp[Q:Q + 8, :]
    xsb[...] = act[:, :D_INNER].astype(BF16)

    dt = _softplus(dt_ref[...] + dtb_ref[...])
    dA = dt * (-jnp.exp(alog_ref[...]))
    row = lax.broadcasted_iota(I32, (Q, Q), 0)
    col = lax.broadcasted_iota(I32, (Q, Q), 1)
    tril = row >= col
    a_cs = jnp.dot(tril.astype(F32), dA, precision=HI, preferred_element_type=F32)
    a_last = a_cs[Q - 1:Q, :]
    w_end = jnp.exp(a_last - a_cs) * dt
    ea = jnp.exp(a_cs)
    cdec = jnp.exp(a_last)
    a_csT = _dot_nt(eye_h, a_cs, HI)
    dtT = _dot_nt(eye_h, dt, HI)
    w_endT = _dot_nt(eye_h, w_end, HI)

    for g in range(SSM_GROUPS):
        b0 = D_INNER + g * SSM_STATE
        c0 = D_INNER + SSM_GROUPS * SSM_STATE + g * SSM_STATE
        Bg = act[:, b0:b0 + SSM_STATE].astype(BF16)
        Cg = act[:, c0:c0 + SSM_STATE].astype(BF16)
        cb = _dot_nt(Cg, Bg)
        BT = _dot_nt(eye_n_bf, Bg)
        Hg = HT[g]
        yoff = _dot(Cg, Hg.astype(BF16))
        for e in range(HPG):
            h = g * HPG + e
            ch = slice(h * SSM_HEAD_DIM, (h + 1) * SSM_HEAD_DIM)
            he = slice(e * SSM_HEAD_DIM, (e + 1) * SSM_HEAD_DIM)
            seg = jnp.broadcast_to(a_cs[:, h:h + 1], (Q, Q)) - a_csT[h:h + 1, :]
            dec = jnp.where(tril, jnp.exp(seg), 0.0)
            M = (cb * dec * dtT[h:h + 1, :]).astype(BF16)
            xh = xsb[:, ch]
            yd = _dot(M, xh)
            yo = yoff[:, he] * jnp.broadcast_to(ea[:, h:h + 1], (Q, SSM_HEAD_DIM))
            ysc[:, ch] = yd + yo
            BTw = (BT * w_endT[h:h + 1, :]).astype(BF16)
            HT[g, :, he] = Hg[:, he] * cdec[:, h:h + 1] + _dot(BTw, xh)

    y = ysc[...] + act[:, :D_INNER] * dfull_ref[...]
    zf = z_ref[...].astype(F32)
    g_ref[...] = _rms_rows(y * (zf * _sigmoid(zf)), nw_ref[...]).astype(g_ref.dtype)

    @pl.when(c == nc - 1)
    def _state_out():
        for h in range(SSM_HEADS):
            g, e = divmod(h, HPG)
            ssmo_ref[h] = _dot_nt(eye_h, HT[g, :, e * SSM_HEAD_DIM:(e + 1) * SSM_HEAD_DIM], HI)


def _ssd_call(z, xbc, dt, prm, *, n_seq, seq_len, row0, Q, conv0=None, ssm0=None, g_in=None):
    T = z.shape[0]
    nc = seq_len // Q
    blk0 = row0 // Q
    has_init = conv0 is not None
    rows = lambda b, c: (blk0 + b * nc + c, 0)
    const = lambda b, c: (0, 0)
    in_specs = [pl.BlockSpec((Q, D_INNER), rows), pl.BlockSpec((Q, CONV_DIM), rows), pl.BlockSpec((Q, SSM_HEADS), rows),
                pl.BlockSpec((CONV_WIDTH, CONV_DIM), const), pl.BlockSpec((1, CONV_DIM), const),
                pl.BlockSpec((1, SSM_HEADS), const), pl.BlockSpec((1, SSM_HEADS), const),
                pl.BlockSpec((1, D_INNER), const), pl.BlockSpec((1, D_INNER), const)]
    args = [z, xbc, dt, prm["conv_w"], prm["conv_b"], prm["dt_bias"], prm["a_log"], prm["d_full"], prm["norm_w"]]
    aliases = {}
    if has_init:
        in_specs += [pl.BlockSpec((None, CONV_WIDTH - 1, CONV_DIM), lambda b, c: (b, 0, 0)),
                     pl.BlockSpec((None, SSM_HEADS, SSM_HEAD_DIM, SSM_STATE), lambda b, c: (b, 0, 0, 0)),
                     pl.BlockSpec(memory_space=pl.ANY)]
        args += [conv0, ssm0, g_in]
        aliases = {len(args) - 1: 0}
    out_shape = (jax.ShapeDtypeStruct((T, D_INNER), BF16),
                 jax.ShapeDtypeStruct((n_seq, CONV_WIDTH - 1, CONV_DIM), F32),
                 jax.ShapeDtypeStruct((n_seq, SSM_HEADS, SSM_HEAD_DIM, SSM_STATE), F32))
    out_specs = (pl.BlockSpec((Q, D_INNER), rows),
                 pl.BlockSpec((None, CONV_WIDTH - 1, CONV_DIM), lambda b, c: (b, 0, 0)),
                 pl.BlockSpec((None, SSM_HEADS, SSM_HEAD_DIM, SSM_STATE), lambda b, c: (b, 0, 0, 0)))
    scratch = [pltpu.VMEM((Q + 8, CONV_DIM), F32),
               pltpu.VMEM((SSM_GROUPS, SSM_STATE, D_INNER // SSM_GROUPS), F32),
               pltpu.VMEM((Q, D_INNER), F32), pltpu.VMEM((Q, CONV_DIM), F32), pltpu.VMEM((Q, D_INNER), BF16)]
    return pl.pallas_call(
        functools.partial(_ssd_body, Q=Q, nc=nc, has_init=has_init), out_shape=out_shape,
        grid=(n_seq, nc), in_specs=in_specs, out_specs=out_specs, scratch_shapes=scratch,
        input_output_aliases=aliases, compiler_params=_params(2, 40),
        name="ssd_sample" if has_init else "ssd_prompt")(*args)


def _attend(q_ref, sink_ref, o_ref, blocks, Lq):
    for kh in range(N_KV_HEADS):
        heads = range(kh * Q_PER_KV, (kh + 1) * Q_PER_KV)
        qs = jnp.concatenate([q_ref[:, h * HEAD_DIM:(h + 1) * HEAD_DIM] for h in heads], axis=0)
        sink = jnp.concatenate([jnp.full((Lq, 1), sink_ref[h], F32) for h in heads], axis=0)
        scores = []
        for k_of, _, bias in blocks:
            s = _dot_nt(qs, k_of(kh).astype(BF16))
            scores.append(s if bias is None else s + bias)
        m = sink
        for s in scores:
            m = jnp.maximum(m, jnp.max(s, axis=-1, keepdims=True))
        es = [jnp.exp(s - m) for s in scores]
        den = jnp.exp(sink - m)
        for e in es:
            den = den + jnp.sum(e, axis=-1, keepdims=True)
        rden = 1.0 / den
        o = None
        for e, (_, v_of, _) in zip(es, blocks):
            part = _dot((e * rden).astype(BF16), v_of(kh).astype(BF16))
            o = part if o is None else o + part
        for i, h in enumerate(heads):
            o_ref[:, h * HEAD_DIM:(h + 1) * HEAD_DIM] = o[i * Lq:(i + 1) * Lq].astype(o_ref.dtype)


def _attn_prompt_body(sink_ref, q_ref, kv0_ref, kv1_ref, kv2_ref, o_ref):
    c = pl.program_id(1)
    KW = N_KV_HEADS * HEAD_DIM
    neg = -jnp.inf
    blocks = []
    for ref, first_valid in ((kv0_ref, 2), (kv1_ref, 1), (kv2_ref, 0)):
        k_of = functools.partial(lambda kh, r: r[:, kh * HEAD_DIM:(kh + 1) * HEAD_DIM], r=ref)
        v_of = functools.partial(lambda kh, r: r[:, KW + kh * HEAD_DIM:KW + (kh + 1) * HEAD_DIM], r=ref)
        bias = None if first_valid == 0 else jnp.where(c >= first_valid, 0.0, neg).astype(F32)
        blocks.append((k_of, v_of, bias))
    _attend(q_ref, sink_ref, o_ref, blocks, CHUNK)


def _attn_sample_body(sink_ref, q_ref, ck_ref, cv_ref, kvn_ref, _, o_ref, *, Lq):
    KW = N_KV_HEADS * HEAD_DIM
    blocks = [(lambda kh: ck_ref[:, kh * HEAD_DIM:(kh + 1) * HEAD_DIM],
               lambda kh: cv_ref[:, kh * HEAD_DIM:(kh + 1) * HEAD_DIM], None),
              (lambda kh: kvn_ref[:, kh * HEAD_DIM:(kh + 1) * HEAD_DIM],
               lambda kh: kvn_ref[:, KW + kh * HEAD_DIM:KW + (kh + 1) * HEAD_DIM], None)]
    _attend(q_ref, sink_ref, o_ref, blocks, Lq)


def _attention(q, kv, sinks, cache_k, cache_v, *, n_prompt, seq, n_dec, dec_len):
    T = q.shape[0]
    NQ = N_Q_HEADS * HEAD_DIM
    KW = N_KV_HEADS * HEAD_DIM
    nc = seq // CHUNK
    smem = pl.BlockSpec(memory_space=pltpu.SMEM)
    band = lambda back: (lambda b, c: (b * nc + jnp.maximum(c - back, 0), 0))
    o = pl.pallas_call(
        _attn_prompt_body, out_shape=jax.ShapeDtypeStruct((T, NQ), BF16), grid=(n_prompt, nc),
        in_specs=[smem, pl.BlockSpec((CHUNK, NQ), lambda b, c: (b * nc + c, 0)),
                  pl.BlockSpec((CHUNK, 2 * KW), band(2)), pl.BlockSpec((CHUNK, 2 * KW), band(1)),
                  pl.BlockSpec((CHUNK, 2 * KW), band(0))],
        out_specs=pl.BlockSpec((CHUNK, NQ), lambda b, c: (b * nc + c, 0)),
        compiler_params=_params(2, 32), name="attn_prompt")(sinks, q, kv, kv, kv)
    blk0 = (n_prompt * seq) // dec_len
    return pl.pallas_call(
        functools.partial(_attn_sample_body, Lq=dec_len), out_shape=jax.ShapeDtypeStruct((T, NQ), BF16), grid=(n_dec,),
        in_specs=[smem, pl.BlockSpec((dec_len, NQ), lambda b: (blk0 + b, 0)),
                  pl.BlockSpec((None, WINDOW, KW), lambda b: (b, 0, 0)),
                  pl.BlockSpec((None, WINDOW, KW), lambda b: (b, 0, 0)),
                  pl.BlockSpec((dec_len, 2 * KW), lambda b: (blk0 + b, 0)),
                  pl.BlockSpec(memory_space=pl.ANY)],
        out_specs=pl.BlockSpec((dec_len, NQ), lambda b: (blk0 + b, 0)),
        input_output_aliases={5: 0}, compiler_params=_params(1, 32), name="attn_sample")(
            sinks, q, cache_k.reshape(n_dec, WINDOW, KW), cache_v.reshape(n_dec, WINDOW, KW), kv, o)


def _router_body(x_ref, g_ref, rt_ref, idx_ref, w_ref):
    h = _rms_rows(x_ref[...], g_ref[...])
    lt = _dot_nt(rt_ref[...], h, HI)
    ids = lax.broadcasted_iota(I32, lt.shape, 0)
    m1 = jnp.max(lt, axis=0, keepdims=True)
    i1 = jnp.min(jnp.where(lt == m1, ids, N_EXPERTS), axis=0, keepdims=True)
    rest = jnp.where(ids == i1, -jnp.inf, lt)
    m2 = jnp.max(rest, axis=0, keepdims=True)
    i2 = jnp.min(jnp.where(rest == m2, ids, N_EXPERTS), axis=0, keepdims=True)
    e2 = jnp.exp(m2 - m1)
    w1 = 1.0 / (1.0 + e2)
    idx_ref[...] = jnp.concatenate([i1, i2], axis=0)
    w_ref[...] = jnp.concatenate([w1, e2 * w1], axis=0)


def _router(x, g, router):
    T = x.shape[0]
    tm = _pick(T, 512)
    return pl.pallas_call(
        _router_body, out_shape=(jax.ShapeDtypeStruct((2, T), I32), jax.ShapeDtypeStruct((2, T), F32)),
        grid=(T // tm,),
        in_specs=[pl.BlockSpec((tm, D_MODEL), lambda i: (i, 0)), pl.BlockSpec((1, D_MODEL), lambda i: (0, 0)),
                  pl.BlockSpec((N_EXPERTS, D_MODEL), lambda i: (0, 0))],
        out_specs=(pl.BlockSpec((2, tm), lambda i: (0, i)), pl.BlockSpec((2, tm), lambda i: (0, i))),
        compiler_params=_params(1, 32), name="moe_router")(x, g.reshape(1, D_MODEL), router.T)


def _route_plan(idx, tm, n_tiles):
    T = idx.shape[1]
    e_flat = idx.reshape(-1)
    onehot = (e_flat[:, None] == jnp.arange(N_EXPERTS, dtype=I32)[None, :]).astype(I32)
    csum = jnp.cumsum(onehot, axis=0)
    counts = csum[-1]
    rank = jnp.take_along_axis(csum, e_flat[:, None], axis=1)[:, 0] - 1
    padded = ((counts + tm - 1) // tm) * tm
    ends = jnp.cumsum(padded)
    pos = (ends - padded)[e_flat] + rank
    tok = jnp.tile(jnp.arange(T, dtype=I32), 2)
    src = jnp.zeros((n_tiles * tm,), I32).at[pos].set(tok)
    tile_start = jnp.arange(n_tiles, dtype=I32) * tm
    valid = tile_start < ends[-1]
    expert = jnp.searchsorted(ends, jnp.minimum(tile_start, ends[-1] - 1), side="right").astype(I32)
    return pos.reshape(2, T).astype(I32), src, jnp.minimum(expert, N_EXPERTS - 1), valid.astype(I32)


def _gather_rows(src_ref, n_rows, table_hbm, buf, sem):
    def copy(r, row):
        return pltpu.make_async_copy(table_hbm.at[pl.ds(row, 1)], buf.at[pl.ds(r, 1)], sem)

    def start(r, carry):
        copy(r, src_ref(r)).start()
        return carry

    def wait(r, carry):
        copy(r, 0).wait()
        return carry

    lax.fori_loop(0, n_rows, start, 0)
    lax.fori_loop(0, n_rows, wait, 0)


def _moe_gather_body(src_ref, x_hbm, g_ref, o_ref, buf, sem):
    tm = buf.shape[0]
    _gather_rows(lambda r: src_ref[0, 0, r], tm, x_hbm, buf, sem)
    o_ref[...] = _rms_rows(buf[...], g_ref[...]).astype(o_ref.dtype)


def _moe_gather(x, g, src, tm):
    n_tiles = src.shape[0] // tm
    return pl.pallas_call(
        _moe_gather_body, out_shape=jax.ShapeDtypeStruct((n_tiles * tm, D_MODEL), BF16), grid=(n_tiles,),
        in_specs=[pl.BlockSpec((1, 1, tm), lambda i: (i, 0, 0), memory_space=pltpu.SMEM),
                  pl.BlockSpec(memory_space=pl.ANY), pl.BlockSpec((1, D_MODEL), lambda i: (0, 0))],
        out_specs=pl.BlockSpec((tm, D_MODEL), lambda i: (i, 0)),
        scratch_shapes=[pltpu.VMEM((tm, D_MODEL), F32), pltpu.SemaphoreType.DMA(())],
        compiler_params=_params(1, 32), name="moe_gather")(src.reshape(n_tiles, 1, tm), x, g.reshape(1, D_MODEL))


def _moe_up_body(te_ref, tv_ref, x_ref, wg_ref, wu_ref, o_ref):
    i = pl.program_id(1)

    @pl.when(tv_ref[i] != 0)
    def _():
        x = x_ref[...]
        g = _dot(x, wg_ref[...])
        o_ref[...] = (g * _sigmoid(g) * _dot(x, wu_ref[...])).astype(o_ref.dtype)

    @pl.when(tv_ref[i] == 0)
    def _():
        o_ref[...] = jnp.zeros(o_ref.shape, o_ref.dtype)


def _moe_down_body(te_ref, tv_ref, x_ref, w_ref, o_ref):
    i = pl.program_id(1)

    @pl.when(tv_ref[i] != 0)
    def _():
        o_ref[...] = _dot(x_ref[...], w_ref[...])

    @pl.when(tv_ref[i] == 0)
    def _():
        o_ref[...] = jnp.zeros(o_ref.shape, o_ref.dtype)


def _moe_experts(xs, w_gate_up, w_down, tile_expert, tile_valid, tm):
    P = xs.shape[0]
    n_tiles = P // tm
    tn = _pick(EXPERT_FF, 512)
    up_blocks = EXPERT_FF // tn
    act = pl.pallas_call(
        _moe_up_body, out_shape=jax.ShapeDtypeStruct((P, EXPERT_FF), BF16),
        grid_spec=pltpu.PrefetchScalarGridSpec(
            num_scalar_prefetch=2, grid=(up_blocks, n_tiles),
            in_specs=[pl.BlockSpec((tm, D_MODEL), lambda j, i, te, tv: (i, 0)),
                      pl.BlockSpec((None, D_MODEL, tn), lambda j, i, te, tv: (te[i], 0, j)),
                      pl.BlockSpec((None, D_MODEL, tn), lambda j, i, te, tv: (te[i], 0, j + up_blocks))],
            out_specs=pl.BlockSpec((tm, tn), lambda j, i, te, tv: (i, j))),
        compiler_params=_params(2, 40), name="moe_up")(tile_expert, tile_valid, xs, w_gate_up, w_gate_up)
    tn = _pick(D_MODEL, 512)
    return pl.pallas_call(
        _moe_down_body, out_shape=jax.ShapeDtypeStruct((P, D_MODEL), F32),
        grid_spec=pltpu.PrefetchScalarGridSpec(
            num_scalar_prefetch=2, grid=(D_MODEL // tn, n_tiles),
            in_specs=[pl.BlockSpec((tm, EXPERT_FF), lambda j, i, te, tv: (i, 0)),
                      pl.BlockSpec((None, EXPERT_FF, tn), lambda j, i, te, tv: (te[i], 0, j))],
            out_specs=pl.BlockSpec((tm, tn), lambda j, i, te, tv: (i, j))),
        compiler_params=_params(2, 48), name="moe_down")(tile_expert, tile_valid, act, w_down)


def _moe_combine_body(pos_ref, x_ref, w_ref, ys_hbm, g_ref, xo_ref, ho_ref, buf0, buf1, sem):
    tc = buf0.shape[0]
    _gather_rows(lambda r: pos_ref[0, 0, r], tc, ys_hbm, buf0, sem)
    _gather_rows(lambda r: pos_ref[0, 1, r], tc, ys_hbm, buf1, sem)
    w = w_ref[...]
    x = x_ref[...] + (w[:, 0:1] * buf0[...] + w[:, 1:2] * buf1[...])
    xo_ref[...] = x
    ho_ref[...] = _rms_rows(x, g_ref[...]).astype(ho_ref.dtype)


def _moe_combine(x, ys, pos, w, g):
    T = x.shape[0]
    tc = _pick(T, 256)
    n = T // tc
    pos_t = pos.reshape(2, n, tc).transpose(1, 0, 2)
    return pl.pallas_call(
        _moe_combine_body,
        out_shape=(jax.ShapeDtypeStruct((T, D_MODEL), F32), jax.ShapeDtypeStruct((T, D_MODEL), BF16)),
        grid=(n,),
        in_specs=[pl.BlockSpec((1, 2, tc), lambda i: (i, 0, 0), memory_space=pltpu.SMEM),
                  pl.BlockSpec((tc, D_MODEL), lambda i: (i, 0)), pl.BlockSpec((tc, 2), lambda i: (i, 0)),
                  pl.BlockSpec(memory_space=pl.ANY), pl.BlockSpec((1, D_MODEL), lambda i: (0, 0))],
        out_specs=(pl.BlockSpec((tc, D_MODEL), lambda i: (i, 0)), pl.BlockSpec((tc, D_MODEL), lambda i: (i, 0))),
        scratch_shapes=[pltpu.VMEM((tc, D_MODEL), F32), pltpu.VMEM((tc, D_MODEL), F32), pltpu.SemaphoreType.DMA(())],
        compiler_params=_params(1, 32), name="moe_combine")(pos_t, x, w.T, ys, g.reshape(1, D_MODEL))


def _rope_tables(pos):
    half = HEAD_DIM // 2
    inv_freq = ROPE_THETA ** (-jnp.arange(half, dtype=F32) * 2.0 / HEAD_DIM)
    ang = pos.astype(F32)[:, None] * inv_freq[None, :]
    cos, sin = jnp.cos(ang), jnp.sin(ang)
    reps = LANES // HEAD_DIM
    return jnp.tile(jnp.concatenate([cos, cos], axis=-1), (1, reps)), jnp.tile(jnp.concatenate([-sin, sin], axis=-1), (1, reps))


def kernel(x_prompt, x_sample, p_prompt, p_sample, cache_conv, state_ssm, cache_k, cache_v, norm_mix, norm_ffn, norm_ple, norm_final, mamba_in_proj, mamba_conv_w, mamba_conv_b, mamba_dt_bias, mamba_a_log, mamba_d, mamba_norm, mamba_out_proj, norm_kv, w_kv, w_q, attn_sinks, w_o, ffn_w_gate_up, ffn_w_down, moe_router, moe_w_gate_up, moe_w_down, ple_proj, ple_gate):
    n_prompt, seq, _ = x_prompt.shape
    n_dec, dec_len, _ = x_sample.shape
    Tp = n_prompt * seq
    T = Tp + n_dec * dec_len
    assert seq % CHUNK == 0 and Tp % dec_len == 0 and dec_len % 8 == 0

    bf = lambda w: w.astype(BF16)
    x0 = jnp.concatenate([x_prompt.reshape(Tp, D_MODEL), x_sample.reshape(-1, D_MODEL)], axis=0)
    p_all = jnp.concatenate([p_prompt.reshape(2, Tp, PLE_DIM), p_sample.reshape(2, -1, PLE_DIM)], axis=1).astype(BF16)
    pos = jnp.concatenate([jnp.tile(jnp.arange(seq, dtype=I32), n_prompt),
                           PAST_LEN + jnp.tile(jnp.arange(dec_len, dtype=I32), n_dec)])
    cos_t, sin_t = _rope_tables(pos)

    tm_wide = _pick(T, 1536)
    tm_mid = _pick(T, 768)
    tm_deep = _pick(T, 512)

    w_in = bf(mamba_in_proj[0])
    h = _rmsnorm(x0, norm_mix[0], BF16)
    tn = _pick(D_INNER, 1024)
    z = _mm_call(_mm_plain_body, T, D_INNER, tm_wide, tn, [("lhs", h), ("w", w_in, 0)], BF16, 48, "in_proj_z")
    xbc = _mm_call(_mm_plain_body, T, CONV_DIM, tm_wide, tn, [("lhs", h), ("w", w_in, D_INNER // tn)], F32, 48, "in_proj_xbc")
    dt = _mm_call(_mm_plain_body, T, SSM_HEADS, tm_wide, SSM_HEADS,
                  [("lhs", h), ("w", w_in[:, D_INNER + CONV_DIM:], 0)], F32, 32, "in_proj_dt")
    ssd_prm = dict(conv_w=mamba_conv_w[0], conv_b=mamba_conv_b[0].reshape(1, CONV_DIM),
                   dt_bias=mamba_dt_bias[0].reshape(1, SSM_HEADS), a_log=mamba_a_log[0].reshape(1, SSM_HEADS),
                   d_full=jnp.repeat(mamba_d[0], SSM_HEAD_DIM).reshape(1, D_INNER),
                   norm_w=mamba_norm[0].reshape(1, D_INNER))
    g, conv_prompt, ssm_prompt = _ssd_call(z, xbc, dt, ssd_prm, n_seq=n_prompt, seq_len=seq, row0=0, Q=CHUNK)
    g, conv_sample, ssm_sample = _ssd_call(z, xbc, dt, ssd_prm, n_seq=n_dec, seq_len=dec_len, row0=Tp, Q=dec_len,
                                           conv0=cache_conv[0], ssm0=state_ssm[0], g_in=g)
    tn = _pick(D_MODEL, 512)
    x1 = _mm_call(_mm_res_body, T, D_MODEL, tm_mid, tn,
                  [("lhs", g), ("w", bf(mamba_out_proj[0]), 0), ("row", x0, True)], F32, 48, "out_proj")

    h = _rmsnorm(x1, norm_ffn[0], BF16)
    tn = _pick(D_FF, 512)
    w_gu = bf(ffn_w_gate_up[0])
    a = _mm_call(_mm_swiglu_body, T, D_FF, tm_wide, tn,
                 [("lhs", h), ("w", w_gu, 0), ("w", w_gu, D_FF // tn)], BF16, 48, "ffn_up")
    tn = _pick(D_MODEL, 512)
    x2 = _mm_call(_mm_res_body, T, D_MODEL, tm_deep, tn,
                  [("lhs", a), ("w", bf(ffn_w_down[0]), 0), ("row", x1, True)], F32, 48, "ffn_down")
    h = _rmsnorm(x2, norm_ple[0], BF16)
    tn = _pick(D_MODEL, 1024)
    x3 = _mm_call(_mm_ple_body, T, D_MODEL, tm_mid, tn,
                  [("lhs", h), ("w", bf(ple_gate[0]), 0), ("lhs", p_all[0]), ("w", bf(ple_proj[0]), 0), ("row", x2, True)],
                  F32, 48, "ple0")

    KW = N_KV_HEADS * HEAD_DIM
    hk = _rmsnorm(x3, norm_kv, BF16)
    kv = _mm_call(functools.partial(_mm_rope_body, n_rope=KW // LANES, scale=1.0), T, 2 * KW, tm_wide, 2 * KW,
                  [("lhs", hk), ("w", bf(w_kv), 0), ("row", cos_t, False), ("row", sin_t, False)], F32, 48, "kv_proj")
    hq = _rmsnorm(x3, norm_mix[1], BF16)
    tn = _pick(D_MODEL, 512)
    q = _mm_call(functools.partial(_mm_rope_body, n_rope=tn // LANES, scale=HEAD_DIM ** -0.5), T, D_MODEL, tm_wide, tn,
                 [("lhs", hq), ("w", bf(w_q[0]), 0), ("row", cos_t, False), ("row", sin_t, False)], BF16, 48, "q_proj")
    o = _attention(q, kv, attn_sinks[0], cache_k, cache_v, n_prompt=n_prompt, seq=seq, n_dec=n_dec, dec_len=dec_len)
    tn = _pick(D_MODEL, 1024)
    x4 = _mm_call(_mm_res_body, T, D_MODEL, tm_mid, tn,
                  [("lhs", o), ("w", bf(w_o[0]), 0), ("row", x3, True)], F32, 48, "o_proj")

    tm_moe = _pick(T, 512)
    n_tiles = (2 * T) // tm_moe + N_EXPERTS
    idx, gate_w = _router(x4, norm_ffn[1], moe_router[0])
    pos_sorted, src, tile_expert, tile_valid = _route_plan(idx, tm_moe, n_tiles)
    xs = _moe_gather(x4, norm_ffn[1], src, tm_moe)
    ys = _moe_experts(xs, bf(moe_w_gate_up[0]), bf(moe_w_down[0]), tile_expert, tile_valid, tm_moe)
    x5, h = _moe_combine(x4, ys, pos_sorted, gate_w, norm_ple[1])
    tn = _pick(D_MODEL, 1024)
    x6 = _mm_call(_mm_ple_body, T, D_MODEL, tm_mid, tn,
                  [("lhs", h), ("w", bf(ple_gate[1]), 0), ("lhs", p_all[1]), ("w", bf(ple_proj[1]), 0), ("row", x5, True)],
                  F32, 48, "ple1")
    y = _rmsnorm(x6, norm_final, F32)

    k_all = kv[:, :KW]
    v_all = kv[:, KW:]
    tail = lambda a: a[:Tp].reshape(n_prompt, seq, N_KV_HEADS, HEAD_DIM)[:, seq - WINDOW:]
    new = lambda a: a[Tp:].reshape(n_dec, dec_len, N_KV_HEADS, HEAD_DIM)
    k_sample = jnp.concatenate([cache_k, new(k_all)], axis=1)[:, -WINDOW:]
    v_sample = jnp.concatenate([cache_v, new(v_all)], axis=1)[:, -WINDOW:]
    return (y[:Tp].reshape(n_prompt, seq, D_MODEL), y[Tp:].reshape(n_dec, dec_len, D_MODEL),
            conv_prompt[None], ssm_prompt[None], tail(k_all), tail(v_all),
            conv_sample[None], ssm_sample[None], k_sample, v_sample)
```

```python
import functools

import jax
import jax.numpy as jnp
from jax import lax
from jax.experimental import pallas as pl
from jax.experimental.pallas import tpu as pltpu

F32 = jnp.float32
BF16 = jnp.bfloat16
I32 = jnp.int32
HI = lax.Precision.HIGHEST

D_MODEL = 2048
D_INNER = 4096
SSM_HEADS = 64
SSM_HEAD_DIM = 64
SSM_GROUPS = 8
SSM_STATE = 128
CONV_WIDTH = 4
CONV_DIM = D_INNER + 2 * SSM_GROUPS * SSM_STATE
N_Q_HEADS = 32
N_KV_HEADS = 4
Q_PER_KV = N_Q_HEADS // N_KV_HEADS
HEAD_DIM = 64
WINDOW = 128
CHUNK = 64
ROPE_THETA = 10000.0
D_FF = 5632
N_EXPERTS = 8
EXPERT_FF = 7168
PLE_DIM = 256
PAST_LEN = 4096
EPS = 1e-6

LANES = 128
MIB = 1024 * 1024
NT_DIMS = (((1,), (1,)), ((), ()))
DMA_UNROLL = 8


def _params(n_axes, vmem_mib):
    return pltpu.CompilerParams(dimension_semantics=("arbitrary",) * n_axes,
                                vmem_limit_bytes=vmem_mib * MIB)


def _pick(n, pref):
    for t in (1536, 1024, 768, 512, 384, 256, 128, 96, 64, 32, 16, 8):
        if t <= pref and n % t == 0:
            return t
    raise ValueError(f"no tile for {n}")


def _dot(a, b):
    return jnp.dot(a, b, preferred_element_type=F32)


def _dot_nt(a, b, precision=None):
    return lax.dot_general(a, b, NT_DIMS, precision=precision, preferred_element_type=F32)


def _sigmoid(x):
    return 1.0 / (1.0 + jnp.exp(-x))


def _softplus(x):
    return jnp.maximum(x, 0.0) + jnp.log1p(jnp.exp(-jnp.abs(x)))


def _eye(n, dtype):
    r = lax.broadcasted_iota(I32, (n, n), 0)
    c = lax.broadcasted_iota(I32, (n, n), 1)
    return (r == c).astype(dtype)


def _rms_rows(x, g):
    inv = lax.rsqrt(jnp.mean(x * x, axis=-1, keepdims=True) + EPS)
    return x * inv * g


def _rms_body(x_ref, g_ref, o_ref):
    o_ref[...] = _rms_rows(x_ref[...], g_ref[...]).astype(o_ref.dtype)


def _rmsnorm(x, g, out_dtype):
    T, D = x.shape
    tr = _pick(T, 512)
    return pl.pallas_call(
        _rms_body, out_shape=jax.ShapeDtypeStruct((T, D), out_dtype), grid=(T // tr,),
        in_specs=[pl.BlockSpec((tr, D), lambda i: (i, 0)), pl.BlockSpec((1, D), lambda i: (0, 0))],
        out_specs=pl.BlockSpec((tr, D), lambda i: (i, 0)),
        compiler_params=_params(1, 32), name="rmsnorm")(x, g.reshape(1, D))


def _mm_plain_body(x_ref, w_ref, o_ref):
    o_ref[...] = _dot(x_ref[...], w_ref[...]).astype(o_ref.dtype)


def _mm_res_body(x_ref, w_ref, r_ref, o_ref):
    o_ref[...] = r_ref[...] + _dot(x_ref[...], w_ref[...])


def _mm_swiglu_body(x_ref, wg_ref, wu_ref, o_ref):
    x = x_ref[...]
    g = _dot(x, wg_ref[...])
    o_ref[...] = (g * _sigmoid(g) * _dot(x, wu_ref[...])).astype(o_ref.dtype)


def _mm_ple_body(h_ref, wg_ref, p_ref, wp_ref, r_ref, o_ref):
    gate = _sigmoid(_dot(h_ref[...], wg_ref[...]))
    o_ref[...] = r_ref[...] + gate * _dot(p_ref[...], wp_ref[...])


def _mm_rope_body(x_ref, w_ref, cos_ref, sin_ref, o_ref, *, n_rope, scale):
    acc = _dot(x_ref[...], w_ref[...])
    tm, tn = acc.shape
    cos = cos_ref[...]
    sin = sin_ref[...]
    lane = lax.broadcasted_iota(I32, (tm, LANES), 1)
    first_half = (lane % HEAD_DIM) < (HEAD_DIM // 2)
    for c in range(tn // LANES):
        x = acc[:, c * LANES:(c + 1) * LANES]
        if c < n_rope:
            partner = jnp.where(first_half, pltpu.roll(x, LANES - HEAD_DIM // 2, 1),
                                pltpu.roll(x, HEAD_DIM // 2, 1))
            x = (x * cos + partner * sin) * scale
        o_ref[:, c * LANES:(c + 1) * LANES] = x.astype(o_ref.dtype)


def _with_weight_cast(body, w_slots, n_in):
    def wrapped(*refs):
        ins, out, scratch = list(refs[:n_in]), refs[n_in], refs[n_in + 1:]

        @pl.when(pl.program_id(1) == 0)
        def _():
            for k, s in zip(w_slots, scratch):
                s[...] = ins[k][...].astype(BF16)

        for k, s in zip(w_slots, scratch):
            ins[k] = s
        body(*ins, out)
    return wrapped


def _mm_call(body, T, N, tm, tn, operands, out_dtype, vmem_mib, name):
    in_specs, args, w_slots, scratch = [], [], [], []
    for op in operands:
        kind, a = op[0], op[1]
        if kind == "lhs":
            in_specs.append(pl.BlockSpec((tm, a.shape[1]), lambda j, i: (i, 0)))
        elif kind == "w":
            in_specs.append(pl.BlockSpec((a.shape[0], tn), functools.partial(lambda j, i, off: (0, j + off), off=op[2])))
            w_slots.append(len(args))
            scratch.append(pltpu.VMEM((a.shape[0], tn), BF16))
        elif op[2]:
            in_specs.append(pl.BlockSpec((tm, tn), lambda j, i: (i, j)))
        else:
            in_specs.append(pl.BlockSpec((tm, a.shape[1]), lambda j, i: (i, 0)))
        args.append(a)
    return pl.pallas_call(
        _with_weight_cast(body, w_slots, len(args)), out_shape=jax.ShapeDtypeStruct((T, N), out_dtype),
        grid=(N // tn, T // tm), in_specs=in_specs, out_specs=pl.BlockSpec((tm, tn), lambda j, i: (i, j)),
        scratch_shapes=scratch, compiler_params=_params(2, vmem_mib), name=name)(*args)


def _ssd_body(*refs, Q, nc, has_init):
    if has_init:
        (z_ref, xbc_ref, dt_ref, cw_ref, cb_ref, dtb_ref, alog_ref, dfull_ref, nw_ref, conv0_ref, ssm0_ref,
         _, g_ref, convo_ref, ssmo_ref, xp, HT, ysc, act, xsb) = refs
    else:
        (z_ref, xbc_ref, dt_ref, cw_ref, cb_ref, dtb_ref, alog_ref, dfull_ref, nw_ref,
         g_ref, convo_ref, ssmo_ref, xp, HT, ysc, act, xsb) = refs
    c = pl.program_id(1)
    HPG = SSM_HEADS // SSM_GROUPS
    eye_h = _eye(SSM_HEADS, F32)
    eye_n_bf = _eye(SSM_STATE, BF16)

    @pl.when(c == 0)
    def _init():
        xp[0:8, :] = jnp.zeros((8, CONV_DIM), F32)
        if has_init:
            xp[8 - (CONV_WIDTH - 1):8, :] = conv0_ref[...]
            for g in range(SSM_GROUPS):
                HT[g] = ssm0_ref[g * HPG:(g + 1) * HPG].reshape(HPG * SSM_HEAD_DIM, SSM_STATE).T
        else:
            HT[...] = jnp.zeros(HT.shape, F32)

    xc = xbc_ref[...]
    xp[8:8 + Q, :] = xc
    cw = cw_ref[...]
    conv = cb_ref[...]
    for t in range(CONV_WIDTH - 1):
        off = 8 - (CONV_WIDTH - 1) + t
        conv = conv + xp[off:off + Q, :] * cw[t:t + 1, :]
    conv = conv + xc * cw[CONV_WIDTH - 1:CONV_WIDTH, :]
    act[...] = conv * _sigmoid(conv)

    @pl.when(c == nc - 1)
    def _conv_out():
        convo_ref[...] = xp[8 + Q - (CONV_WIDTH - 1):8 + Q, :]

    xp[0:8, :] = xp[Q:Q + 8, :]
    xsb[...] = act[:, :D_INNER].astype(BF16)

    dt = _softplus(dt_ref[...] + dtb_ref[...])
    dA = dt * (-jnp.exp(alog_ref[...]))
    row = lax.broadcasted_iota(I32, (Q, Q), 0)
    col = lax.broadcasted_iota(I32, (Q, Q), 1)
    tril = row >= col
    a_cs = jnp.dot(tril.astype(F32), dA, precision=HI, preferred_element_type=F32)
    a_last = a_cs[Q - 1:Q, :]
    w_end = jnp.exp(a_last - a_cs) * dt
    ea = jnp.exp(a_cs)
    cdec = jnp.exp(a_last)
    a_csT = _dot_nt(eye_h, a_cs, HI)
    dtT = _dot_nt(eye_h, dt, HI)
    w_endT = _dot_nt(eye_h, w_end, HI)

    for g in range(SSM_GROUPS):
        b0 = D_INNER + g * SSM_STATE
        c0 = D_INNER + SSM_GROUPS * SSM_STATE + g * SSM_STATE
        Bg = act[:, b0:b0 + SSM_STATE].astype(BF16)
        Cg = act[:, c0:c0 + SSM_STATE].astype(BF16)
        cb = _dot_nt(Cg, Bg)
        BT = _dot_nt(eye_n_bf, Bg)
        Hg = HT[g]
        yoff = _dot(Cg, Hg.astype(BF16))
        for e in range(HPG):
            h = g * HPG + e
            ch = slice(h * SSM_HEAD_DIM, (h + 1) * SSM_HEAD_DIM)
            he = slice(e * SSM_HEAD_DIM, (e + 1) * SSM_HEAD_DIM)
            seg = jnp.broadcast_to(a_cs[:, h:h + 1], (Q, Q)) - a_csT[h:h + 1, :]
            dec = jnp.where(tril, jnp.exp(seg), 0.0)
            M = (cb * dec * dtT[h:h + 1, :]).astype(BF16)
            BTw = (BT * w_endT[h:h + 1, :]).astype(BF16)
            r = _dot(jnp.concatenate([M, BTw], axis=0), xsb[:, ch])
            yo = yoff[:, he] * jnp.broadcast_to(ea[:, h:h + 1], (Q, SSM_HEAD_DIM))
            ysc[:, ch] = r[:Q] + yo
            HT[g, :, he] = Hg[:, he] * cdec[:, h:h + 1] + r[Q:]

    y = ysc[...] + act[:, :D_INNER] * dfull_ref[...]
    zf = z_ref[...].astype(F32)
    g_ref[...] = _rms_rows(y * (zf * _sigmoid(zf)), nw_ref[...]).astype(g_ref.dtype)

    @pl.when(c == nc - 1)
    def _state_out():
        for g in range(SSM_GROUPS):
            ssmo_ref[g * HPG:(g + 1) * HPG] = HT[g].T.reshape(HPG, SSM_HEAD_DIM, SSM_STATE)


def _ssd_call(z, xbc, dt, prm, *, n_seq, seq_len, row0, Q, conv0=None, ssm0=None, g_in=None):
    T = z.shape[0]
    nc = seq_len // Q
    blk0 = row0 // Q
    has_init = conv0 is not None
    rows = lambda b, c: (blk0 + b * nc + c, 0)
    const = lambda b, c: (0, 0)
    in_specs = [pl.BlockSpec((Q, D_INNER), rows), pl.BlockSpec((Q, CONV_DIM), rows), pl.BlockSpec((Q, SSM_HEADS), rows),
                pl.BlockSpec((CONV_WIDTH, CONV_DIM), const), pl.BlockSpec((1, CONV_DIM), const),
                pl.BlockSpec((1, SSM_HEADS), const), pl.BlockSpec((1, SSM_HEADS), const),
                pl.BlockSpec((1, D_INNER), const), pl.BlockSpec((1, D_INNER), const)]
    args = [z, xbc, dt, prm["conv_w"], prm["conv_b"], prm["dt_bias"], prm["a_log"], prm["d_full"], prm["norm_w"]]
    aliases = {}
    if has_init:
        in_specs += [pl.BlockSpec((None, CONV_WIDTH - 1, CONV_DIM), lambda b, c: (b, 0, 0)),
                     pl.BlockSpec((None, SSM_HEADS, SSM_HEAD_DIM, SSM_STATE), lambda b, c: (b, 0, 0, 0)),
                     pl.BlockSpec(memory_space=pl.ANY)]
        args += [conv0, ssm0, g_in]
        aliases = {len(args) - 1: 0}
    out_shape = (jax.ShapeDtypeStruct((T, D_INNER), BF16),
                 jax.ShapeDtypeStruct((n_seq, CONV_WIDTH - 1, CONV_DIM), F32),
                 jax.ShapeDtypeStruct((n_seq, SSM_HEADS, SSM_HEAD_DIM, SSM_STATE), F32))
    out_specs = (pl.BlockSpec((Q, D_INNER), rows),
                 pl.BlockSpec((None, CONV_WIDTH - 1, CONV_DIM), lambda b, c: (b, 0, 0)),
                 pl.BlockSpec((None, SSM_HEADS, SSM_HEAD_DIM, SSM_STATE), lambda b, c: (b, 0, 0, 0)))
    scratch = [pltpu.VMEM((Q + 8, CONV_DIM), F32),
               pltpu.VMEM((SSM_GROUPS, SSM_STATE, D_INNER // SSM_GROUPS), F32),
               pltpu.VMEM((Q, D_INNER), F32), pltpu.VMEM((Q, CONV_DIM), F32), pltpu.VMEM((Q, D_INNER), BF16)]
    return pl.pallas_call(
        functools.partial(_ssd_body, Q=Q, nc=nc, has_init=has_init), out_shape=out_shape,
        grid=(n_seq, nc), in_specs=in_specs, out_specs=out_specs, scratch_shapes=scratch,
        input_output_aliases=aliases, compiler_params=_params(2, 40),
        name="ssd_sample" if has_init else "ssd_prompt")(*args)


def _attend(q_ref, sink_ref, o_ref, kband, vband, bias):
    ones = jnp.ones((kband.shape[1], HEAD_DIM), BF16)
    for h in range(N_Q_HEADS):
        kh = h // Q_PER_KV
        hd = slice(h * HEAD_DIM, (h + 1) * HEAD_DIM)
        s = _dot_nt(q_ref[:, hd], kband[kh])
        if bias is not None:
            s = s + bias
        sink = sink_ref[h]
        m = jnp.maximum(jnp.max(s, axis=-1, keepdims=True), sink)
        e = jnp.exp(s - m).astype(BF16)
        den = _dot(e, ones) + jnp.exp(sink - m)
        o_ref[:, hd] = (_dot(e, vband[kh]) / den).astype(o_ref.dtype)


def _stage_kv(kband, vband, row0, k_of, v_of):
    n = k_of(0).shape[0]
    for kh in range(N_KV_HEADS):
        kband[kh, row0:row0 + n, :] = k_of(kh).astype(BF16)
        vband[kh, row0:row0 + n, :] = v_of(kh).astype(BF16)


def _attn_prompt_body(sink_ref, q_ref, kv0_ref, kv1_ref, kv2_ref, o_ref, kband, vband):
    c = pl.program_id(1)
    KW = N_KV_HEADS * HEAD_DIM
    for j, ref in enumerate((kv0_ref, kv1_ref, kv2_ref)):
        _stage_kv(kband, vband, j * CHUNK,
                  functools.partial(lambda kh, r: r[:, kh * HEAD_DIM:(kh + 1) * HEAD_DIM], r=ref),
                  functools.partial(lambda kh, r: r[:, KW + kh * HEAD_DIM:KW + (kh + 1) * HEAD_DIM], r=ref))
    key = lax.broadcasted_iota(I32, (1, kband.shape[1]), 1)
    bias = jnp.where(key >= jnp.maximum(2 - c, 0) * CHUNK, 0.0, -jnp.inf).astype(F32)
    _attend(q_ref, sink_ref, o_ref, kband, vband, bias)


def _attn_sample_body(sink_ref, q_ref, ck_ref, cv_ref, kvn_ref, _, o_ref, kband, vband):
    KW = N_KV_HEADS * HEAD_DIM
    _stage_kv(kband, vband, 0, lambda kh: ck_ref[:, kh * HEAD_DIM:(kh + 1) * HEAD_DIM],
              lambda kh: cv_ref[:, kh * HEAD_DIM:(kh + 1) * HEAD_DIM])
    _stage_kv(kband, vband, WINDOW, lambda kh: kvn_ref[:, kh * HEAD_DIM:(kh + 1) * HEAD_DIM],
              lambda kh: kvn_ref[:, KW + kh * HEAD_DIM:KW + (kh + 1) * HEAD_DIM])
    _attend(q_ref, sink_ref, o_ref, kband, vband, None)


def _attention(q, kv, sinks, cache_k, cache_v, *, n_prompt, seq, n_dec, dec_len):
    T = q.shape[0]
    NQ = N_Q_HEADS * HEAD_DIM
    KW = N_KV_HEADS * HEAD_DIM
    nc = seq // CHUNK
    smem = pl.BlockSpec(memory_space=pltpu.SMEM)
    band = lambda back: (lambda b, c: (b * nc + jnp.maximum(c - back, 0), 0))
    bands = lambda keys: [pltpu.VMEM((N_KV_HEADS, keys, HEAD_DIM), BF16)] * 2
    o = pl.pallas_call(
        _attn_prompt_body, out_shape=jax.ShapeDtypeStruct((T, NQ), BF16), grid=(n_prompt, nc),
        in_specs=[smem, pl.BlockSpec((CHUNK, NQ), lambda b, c: (b * nc + c, 0)),
                  pl.BlockSpec((CHUNK, 2 * KW), band(2)), pl.BlockSpec((CHUNK, 2 * KW), band(1)),
                  pl.BlockSpec((CHUNK, 2 * KW), band(0))],
        out_specs=pl.BlockSpec((CHUNK, NQ), lambda b, c: (b * nc + c, 0)),
        scratch_shapes=bands(WINDOW + CHUNK),
        compiler_params=_params(2, 32), name="attn_prompt")(sinks, q, kv, kv, kv)
    blk0 = (n_prompt * seq) // dec_len
    return pl.pallas_call(
        _attn_sample_body, out_shape=jax.ShapeDtypeStruct((T, NQ), BF16), grid=(n_dec,),
        in_specs=[smem, pl.BlockSpec((dec_len, NQ), lambda b: (blk0 + b, 0)),
                  pl.BlockSpec((None, WINDOW, KW), lambda b: (b, 0, 0)),
                  pl.BlockSpec((None, WINDOW, KW), lambda b: (b, 0, 0)),
                  pl.BlockSpec((dec_len, 2 * KW), lambda b: (blk0 + b, 0)),
                  pl.BlockSpec(memory_space=pl.ANY)],
        out_specs=pl.BlockSpec((dec_len, NQ), lambda b: (blk0 + b, 0)),
        scratch_shapes=bands(WINDOW + dec_len),
        input_output_aliases={5: 0}, compiler_params=_params(1, 32), name="attn_sample")(
            sinks, q, cache_k.reshape(n_dec, WINDOW, KW), cache_v.reshape(n_dec, WINDOW, KW), kv, o)


def _router_body(x_ref, g_ref, rt_ref, idx_ref, w_ref):
    h = _rms_rows(x_ref[...], g_ref[...])
    lt = _dot_nt(rt_ref[...], h, HI)
    ids = lax.broadcasted_iota(I32, lt.shape, 0)
    m1 = jnp.max(lt, axis=0, keepdims=True)
    i1 = jnp.min(jnp.where(lt == m1, ids, N_EXPERTS), axis=0, keepdims=True)
    rest = jnp.where(ids == i1, -jnp.inf, lt)
    m2 = jnp.max(rest, axis=0, keepdims=True)
    i2 = jnp.min(jnp.where(rest == m2, ids, N_EXPERTS), axis=0, keepdims=True)
    e2 = jnp.exp(m2 - m1)
    w1 = 1.0 / (1.0 + e2)
    idx_ref[...] = jnp.concatenate([i1, i2], axis=0)
    w_ref[...] = jnp.concatenate([w1, e2 * w1], axis=0)


def _router(x, g, router):
    T = x.shape[0]
    tm = _pick(T, 512)
    return pl.pallas_call(
        _router_body, out_shape=(jax.ShapeDtypeStruct((2, T), I32), jax.ShapeDtypeStruct((2, T), F32)),
        grid=(T // tm,),
        in_specs=[pl.BlockSpec((tm, D_MODEL), lambda i: (i, 0)), pl.BlockSpec((1, D_MODEL), lambda i: (0, 0)),
                  pl.BlockSpec((N_EXPERTS, D_MODEL), lambda i: (0, 0))],
        out_specs=(pl.BlockSpec((2, tm), lambda i: (0, i)), pl.BlockSpec((2, tm), lambda i: (0, i))),
        compiler_params=_params(1, 32), name="moe_router")(x, g.reshape(1, D_MODEL), router.T)


def _route_plan(idx, tm, n_tiles):
    T = idx.shape[1]
    e_flat = idx.reshape(-1)
    onehot = (e_flat[:, None] == jnp.arange(N_EXPERTS, dtype=I32)[None, :]).astype(I32)
    csum = jnp.cumsum(onehot, axis=0)
    counts = csum[-1]
    rank = jnp.sum(csum * onehot, axis=1) - 1
    padded = ((counts + tm - 1) // tm) * tm
    ends = jnp.cumsum(padded)
    pos = jnp.sum((ends - padded)[None, :] * onehot, axis=1) + rank
    tok = jnp.tile(jnp.arange(T, dtype=I32), 2)
    src = jnp.zeros((n_tiles * tm,), I32).at[pos].set(tok)
    tile_start = jnp.arange(n_tiles, dtype=I32) * tm
    valid = tile_start < ends[-1]
    probe = jnp.minimum(tile_start, ends[-1] - 1)
    expert = jnp.minimum(jnp.sum((ends[None, :] <= probe[:, None]).astype(I32), axis=1), N_EXPERTS - 1)
    first = jnp.concatenate([jnp.ones((1,), I32), (expert[1:] != expert[:-1]).astype(I32)])
    return pos.reshape(2, T).astype(I32), src, expert, valid.astype(I32), first


def _row_copy(table_hbm, row, buf, r, sem):
    return pltpu.make_async_copy(table_hbm.at[pl.ds(row, 1)], buf.at[pl.ds(r, 1)], sem)


def _start_rows(index_of, n_rows, table_hbm, buf, sem):
    def start(r, carry):
        _row_copy(table_hbm, index_of(r), buf, r, sem).start()
        return carry
    lax.fori_loop(0, n_rows, start, 0, unroll=DMA_UNROLL)


def _wait_rows(n_rows, table_hbm, buf, sem):
    def wait(r, carry):
        _row_copy(table_hbm, 0, buf, r, sem).wait()
        return carry
    lax.fori_loop(0, n_rows, wait, 0, unroll=DMA_UNROLL)


def _moe_gather_body(src_ref, nxt_ref, x_hbm, g_ref, o_ref, buf, sem):
    i = pl.program_id(0)
    tm = buf.shape[1]
    slot = i % 2

    @pl.when(i == 0)
    def _():
        _start_rows(lambda r: src_ref[0, 0, r], tm, x_hbm, buf.at[0], sem.at[0])

    @pl.when(i + 1 < pl.num_programs(0))
    def _():
        _start_rows(lambda r: nxt_ref[0, 0, r], tm, x_hbm, buf.at[1 - slot], sem.at[1 - slot])

    _wait_rows(tm, x_hbm, buf.at[slot], sem.at[slot])
    o_ref[...] = _rms_rows(buf[slot], g_ref[...]).astype(o_ref.dtype)


def _moe_gather(x, g, src, tm):
    n_tiles = src.shape[0] // tm
    src3 = src.reshape(n_tiles, 1, tm)
    return pl.pallas_call(
        _moe_gather_body, out_shape=jax.ShapeDtypeStruct((n_tiles * tm, D_MODEL), BF16), grid=(n_tiles,),
        in_specs=[pl.BlockSpec((1, 1, tm), lambda i: (i, 0, 0), memory_space=pltpu.SMEM),
                  pl.BlockSpec((1, 1, tm), lambda i: (jnp.minimum(i + 1, n_tiles - 1), 0, 0), memory_space=pltpu.SMEM),
                  pl.BlockSpec(memory_space=pl.ANY), pl.BlockSpec((1, D_MODEL), lambda i: (0, 0))],
        out_specs=pl.BlockSpec((tm, D_MODEL), lambda i: (i, 0)),
        scratch_shapes=[pltpu.VMEM((2, tm, D_MODEL), F32), pltpu.SemaphoreType.DMA((2,))],
        compiler_params=_params(1, 32), name="moe_gather")(src3, src3, x, g.reshape(1, D_MODEL))


def _moe_up_body(te_ref, tv_ref, tf_ref, x_ref, wg_ref, wu_ref, o_ref, wg_bf, wu_bf):
    i = pl.program_id(1)

    @pl.when(tf_ref[i] != 0)
    def _():
        wg_bf[...] = wg_ref[...].astype(BF16)
        wu_bf[...] = wu_ref[...].astype(BF16)

    @pl.when(tv_ref[i] != 0)
    def _():
        x = x_ref[...]
        g = _dot(x, wg_bf[...])
        o_ref[...] = (g * _sigmoid(g) * _dot(x, wu_bf[...])).astype(o_ref.dtype)

    @pl.when(tv_ref[i] == 0)
    def _():
        o_ref[...] = jnp.zeros(o_ref.shape, o_ref.dtype)


def _moe_down_body(te_ref, tv_ref, x_ref, w_ref, o_ref):
    i = pl.program_id(1)

    @pl.when(tv_ref[i] != 0)
    def _():
        o_ref[...] = _dot(x_ref[...], w_ref[...])

    @pl.when(tv_ref[i] == 0)
    def _():
        o_ref[...] = jnp.zeros(o_ref.shape, o_ref.dtype)


def _moe_experts(xs, w_gate_up, w_down, tile_expert, tile_valid, tile_first, tm):
    P = xs.shape[0]
    n_tiles = P // tm
    tn = _pick(EXPERT_FF, 512)
    up_blocks = EXPERT_FF // tn
    act = pl.pallas_call(
        _moe_up_body, out_shape=jax.ShapeDtypeStruct((P, EXPERT_FF), BF16),
        grid_spec=pltpu.PrefetchScalarGridSpec(
            num_scalar_prefetch=3, grid=(up_blocks, n_tiles),
            in_specs=[pl.BlockSpec((tm, D_MODEL), lambda j, i, te, tv, tf: (i, 0)),
                      pl.BlockSpec((None, D_MODEL, tn), lambda j, i, te, tv, tf: (te[i], 0, j)),
                      pl.BlockSpec((None, D_MODEL, tn), lambda j, i, te, tv, tf: (te[i], 0, j + up_blocks))],
            out_specs=pl.BlockSpec((tm, tn), lambda j, i, te, tv, tf: (i, j)),
            scratch_shapes=[pltpu.VMEM((D_MODEL, tn), BF16)] * 2),
        compiler_params=_params(2, 40), name="moe_up")(tile_expert, tile_valid, tile_first, xs, w_gate_up, w_gate_up)
    tn = _pick(D_MODEL, 512)
    return pl.pallas_call(
        _moe_down_body, out_shape=jax.ShapeDtypeStruct((P, D_MODEL), F32),
        grid_spec=pltpu.PrefetchScalarGridSpec(
            num_scalar_prefetch=2, grid=(D_MODEL // tn, n_tiles),
            in_specs=[pl.BlockSpec((tm, EXPERT_FF), lambda j, i, te, tv: (i, 0)),
                      pl.BlockSpec((None, EXPERT_FF, tn), lambda j, i, te, tv: (te[i], 0, j))],
            out_specs=pl.BlockSpec((tm, tn), lambda j, i, te, tv: (i, j))),
        compiler_params=_params(2, 48), name="moe_down")(tile_expert, tile_valid, act, w_down)


def _moe_combine_body(pos_ref, nxt_ref, x_ref, w_ref, ys_hbm, g_ref, xo_ref, ho_ref, buf, sem):
    i = pl.program_id(0)
    tc = buf.shape[2]
    slot = i % 2

    def start(table, s):
        for k in range(2):
            _start_rows(functools.partial(lambda r, k: table[0, k, r], k=k), tc, ys_hbm, buf.at[s, k], sem.at[s])

    @pl.when(i == 0)
    def _():
        start(pos_ref, 0)

    @pl.when(i + 1 < pl.num_programs(0))
    def _():
        start(nxt_ref, 1 - slot)

    for k in range(2):
        _wait_rows(tc, ys_hbm, buf.at[slot, k], sem.at[slot])
    w = w_ref[...]
    x = x_ref[...] + (w[:, 0:1] * buf[slot, 0] + w[:, 1:2] * buf[slot, 1])
    xo_ref[...] = x
    ho_ref[...] = _rms_rows(x, g_ref[...]).astype(ho_ref.dtype)


def _moe_combine(x, ys, pos, w, g):
    T = x.shape[0]
    tc = _pick(T, 256)
    n = T // tc
    pos_t = pos.reshape(2, n, tc).transpose(1, 0, 2)
    return pl.pallas_call(
        _moe_combine_body,
        out_shape=(jax.ShapeDtypeStruct((T, D_MODEL), F32), jax.ShapeDtypeStruct((T, D_MODEL), BF16)),
        grid=(n,),
        in_specs=[pl.BlockSpec((1, 2, tc), lambda i: (i, 0, 0), memory_space=pltpu.SMEM),
                  pl.BlockSpec((1, 2, tc), lambda i: (jnp.minimum(i + 1, n - 1), 0, 0), memory_space=pltpu.SMEM),
                  pl.BlockSpec((tc, D_MODEL), lambda i: (i, 0)), pl.BlockSpec((tc, 2), lambda i: (i, 0)),
                  pl.BlockSpec(memory_space=pl.ANY), pl.BlockSpec((1, D_MODEL), lambda i: (0, 0))],
        out_specs=(pl.BlockSpec((tc, D_MODEL), lambda i: (i, 0)), pl.BlockSpec((tc, D_MODEL), lambda i: (i, 0))),
        scratch_shapes=[pltpu.VMEM((2, 2, tc, D_MODEL), F32), pltpu.SemaphoreType.DMA((2,))],
        compiler_params=_params(1, 40), name="moe_combine")(pos_t, pos_t, x, w.T, ys, g.reshape(1, D_MODEL))


def _rope_tables(pos):
    half = HEAD_DIM // 2
    inv_freq = ROPE_THETA ** (-jnp.arange(half, dtype=F32) * 2.0 / HEAD_DIM)
    ang = pos.astype(F32)[:, None] * inv_freq[None, :]
    cos, sin = jnp.cos(ang), jnp.sin(ang)
    reps = LANES // HEAD_DIM
    return jnp.tile(jnp.concatenate([cos, cos], axis=-1), (1, reps)), jnp.tile(jnp.concatenate([-sin, sin], axis=-1), (1, reps))


def kernel(x_prompt, x_sample, p_prompt, p_sample, cache_conv, state_ssm, cache_k, cache_v, norm_mix, norm_ffn, norm_ple, norm_final, mamba_in_proj, mamba_conv_w, mamba_conv_b, mamba_dt_bias, mamba_a_log, mamba_d, mamba_norm, mamba_out_proj, norm_kv, w_kv, w_q, attn_sinks, w_o, ffn_w_gate_up, ffn_w_down, moe_router, moe_w_gate_up, moe_w_down, ple_proj, ple_gate):
    n_prompt, seq, _ = x_prompt.shape
    n_dec, dec_len, _ = x_sample.shape
    Tp = n_prompt * seq
    T = Tp + n_dec * dec_len
    assert seq % CHUNK == 0 and Tp % dec_len == 0 and dec_len % 8 == 0

    bf = lambda w: w.astype(BF16)
    x0 = jnp.concatenate([x_prompt.reshape(Tp, D_MODEL), x_sample.reshape(-1, D_MODEL)], axis=0)
    p_all = jnp.concatenate([p_prompt.reshape(2, Tp, PLE_DIM), p_sample.reshape(2, -1, PLE_DIM)], axis=1).astype(BF16)
    pos = jnp.concatenate([jnp.tile(jnp.arange(seq, dtype=I32), n_prompt),
                           PAST_LEN + jnp.tile(jnp.arange(dec_len, dtype=I32), n_dec)])
    cos_t, sin_t = _rope_tables(pos)

    tm_wide = _pick(T, 1536)
    tm_mid = _pick(T, 768)
    tm_deep = _pick(T, 512)

    w_in = mamba_in_proj[0]
    h = _rmsnorm(x0, norm_mix[0], BF16)
    tn = _pick(D_INNER, 512)
    z = _mm_call(_mm_plain_body, T, D_INNER, tm_wide, tn, [("lhs", h), ("w", w_in, 0)], BF16, 48, "in_proj_z")
    xbc = _mm_call(_mm_plain_body, T, CONV_DIM, tm_wide, tn, [("lhs", h), ("w", w_in, D_INNER // tn)], F32, 48, "in_proj_xbc")
    dt = _mm_call(_mm_plain_body, T, SSM_HEADS, tm_wide, SSM_HEADS,
                  [("lhs", h), ("w", w_in[:, D_INNER + CONV_DIM:], 0)], F32, 32, "in_proj_dt")
    ssd_prm = dict(conv_w=mamba_conv_w[0], conv_b=mamba_conv_b[0].reshape(1, CONV_DIM),
                   dt_bias=mamba_dt_bias[0].reshape(1, SSM_HEADS), a_log=mamba_a_log[0].reshape(1, SSM_HEADS),
                   d_full=jnp.repeat(mamba_d[0], SSM_HEAD_DIM).reshape(1, D_INNER),
                   norm_w=mamba_norm[0].reshape(1, D_INNER))
    g, conv_prompt, ssm_prompt = _ssd_call(z, xbc, dt, ssd_prm, n_seq=n_prompt, seq_len=seq, row0=0, Q=CHUNK)
    g, conv_sample, ssm_sample = _ssd_call(z, xbc, dt, ssd_prm, n_seq=n_dec, seq_len=dec_len, row0=Tp, Q=dec_len,
                                           conv0=cache_conv[0], ssm0=state_ssm[0], g_in=g)
    tn = _pick(D_MODEL, 512)
    x1 = _mm_call(_mm_res_body, T, D_MODEL, tm_mid, tn,
                  [("lhs", g), ("w", mamba_out_proj[0], 0), ("row", x0, True)], F32, 48, "out_proj")

    h = _rmsnorm(x1, norm_ffn[0], BF16)
    tn = _pick(D_FF, 512)
    w_gu = ffn_w_gate_up[0]
    a = _mm_call(_mm_swiglu_body, T, D_FF, tm_wide, tn,
                 [("lhs", h), ("w", w_gu, 0), ("w", w_gu, D_FF // tn)], BF16, 56, "ffn_up")
    tn = _pick(D_MODEL, 512)
    x2 = _mm_call(_mm_res_body, T, D_MODEL, tm_deep, tn,
                  [("lhs", a), ("w", ffn_w_down[0], 0), ("row", x1, True)], F32, 52, "ffn_down")
    h = _rmsnorm(x2, norm_ple[0], BF16)
    tn = _pick(D_MODEL, 512)
    x3 = _mm_call(_mm_ple_body, T, D_MODEL, tm_mid, tn,
                  [("lhs", h), ("w", ple_gate[0], 0), ("lhs", p_all[0]), ("w", ple_proj[0], 0), ("row", x2, True)],
                  F32, 48, "ple0")

    KW = N_KV_HEADS * HEAD_DIM
    hk = _rmsnorm(x3, norm_kv, BF16)
    kv = _mm_call(functools.partial(_mm_rope_body, n_rope=KW // LANES, scale=1.0), T, 2 * KW, tm_wide, 2 * KW,
                  [("lhs", hk), ("w", w_kv, 0), ("row", cos_t, False), ("row", sin_t, False)], F32, 48, "kv_proj")
    hq = _rmsnorm(x3, norm_mix[1], BF16)
    tn = _pick(D_MODEL, 512)
    q = _mm_call(functools.partial(_mm_rope_body, n_rope=tn // LANES, scale=HEAD_DIM ** -0.5), T, D_MODEL, tm_wide, tn,
                 [("lhs", hq), ("w", w_q[0], 0), ("row", cos_t, False), ("row", sin_t, False)], BF16, 48, "q_proj")
    o = _attention(q, kv, attn_sinks[0], cache_k, cache_v, n_prompt=n_prompt, seq=seq, n_dec=n_dec, dec_len=dec_len)
    tn = _pick(D_MODEL, 1024)
    x4 = _mm_call(_mm_res_body, T, D_MODEL, tm_mid, tn,
                  [("lhs", o), ("w", w_o[0], 0), ("row", x3, True)], F32, 48, "o_proj")

    tm_moe = _pick(T, 512)
    n_tiles = (2 * T) // tm_moe + N_EXPERTS
    idx, gate_w = _router(x4, norm_ffn[1], moe_router[0])
    pos_sorted, src, tile_expert, tile_valid, tile_first = _route_plan(idx, tm_moe, n_tiles)
    xs = _moe_gather(x4, norm_ffn[1], src, tm_moe)
    ys = _moe_experts(xs, moe_w_gate_up[0], bf(moe_w_down[0]), tile_expert, tile_valid, tile_first, tm_moe)
    x5, h = _moe_combine(x4, ys, pos_sorted, gate_w, norm_ple[1])
    tn = _pick(D_MODEL, 512)
    x6 = _mm_call(_mm_ple_body, T, D_MODEL, tm_mid, tn,
                  [("lhs", h), ("w", ple_gate[1], 0), ("lhs", p_all[1]), ("w", ple_proj[1], 0), ("row", x5, True)],
                  F32, 48, "ple1")
    y = _rmsnorm(x6, norm_final, F32)

    k_all = kv[:, :KW]
    v_all = kv[:, KW:]
    tail = lambda a: a[:Tp].reshape(n_prompt, seq, N_KV_HEADS, HEAD_DIM)[:, seq - WINDOW:]
    new = lambda a: a[Tp:].reshape(n_dec, dec_len, N_KV_HEADS, HEAD_DIM)
    k_sample = jnp.concatenate([cache_k, new(k_all)], axis=1)[:, -WINDOW:]
    v_sample = jnp.concatenate([cache_v, new(v_all)], axis=1)[:, -WINDOW:]
    return (y[:Tp].reshape(n_prompt, seq, D_MODEL), y[Tp:].reshape(n_dec, dec_len, D_MODEL),
            conv_prompt[None], ssm_prompt[None], tail(k_all), tail(v_all),
            conv_sample[None], ssm_sample[None], k_sample, v_sample)
```

```python
import functools

import jax
import jax.numpy as jnp
from jax import lax
from jax.experimental import pallas as pl
from jax.experimental.pallas import tpu as pltpu

F32 = jnp.float32
BF16 = jnp.bfloat16
I32 = jnp.int32
HI = lax.Precision.HIGHEST

D_MODEL = 2048
D_INNER = 4096
SSM_HEADS = 64
SSM_HEAD_DIM = 64
SSM_GROUPS = 8
SSM_STATE = 128
CONV_WIDTH = 4
CONV_DIM = D_INNER + 2 * SSM_GROUPS * SSM_STATE
N_Q_HEADS = 32
N_KV_HEADS = 4
Q_PER_KV = N_Q_HEADS // N_KV_HEADS
HEAD_DIM = 64
WINDOW = 128
CHUNK = 64
ROPE_THETA = 10000.0
D_FF = 5632
N_EXPERTS = 8
EXPERT_FF = 7168
PLE_DIM = 256
PAST_LEN = 4096
EPS = 1e-6

LANES = 128
MIB = 1024 * 1024
NT_DIMS = (((1,), (1,)), ((), ()))
DMA_UNROLL = 8
CONV_COLS = 512
PAIR_W = 2 * SSM_HEAD_DIM


def _params(n_axes, vmem_mib):
    return pltpu.CompilerParams(dimension_semantics=("arbitrary",) * n_axes,
                                vmem_limit_bytes=vmem_mib * MIB)


def _pick(n, pref):
    for t in (1536, 1024, 768, 512, 384, 256, 128, 96, 64, 32, 16, 8):
        if t <= pref and n % t == 0:
            return t
    raise ValueError(f"no tile for {n}")


def _dot(a, b):
    return jnp.dot(a, b, preferred_element_type=F32)


def _dot_nt(a, b, precision=None):
    return lax.dot_general(a, b, NT_DIMS, precision=precision, preferred_element_type=F32)


def _sigmoid(x):
    return 1.0 / (1.0 + jnp.exp(-x))


def _softplus(x):
    return jnp.maximum(x, 0.0) + jnp.log1p(jnp.exp(-jnp.abs(x)))


def _eye(n, dtype):
    r = lax.broadcasted_iota(I32, (n, n), 0)
    c = lax.broadcasted_iota(I32, (n, n), 1)
    return (r == c).astype(dtype)


def _rms_rows(x, g):
    inv = lax.rsqrt(jnp.mean(x * x, axis=-1, keepdims=True) + EPS)
    return x * inv * g


def _rms_body(x_ref, g_ref, o_ref):
    o_ref[...] = _rms_rows(x_ref[...], g_ref[...]).astype(o_ref.dtype)


def _rmsnorm(x, g, out_dtype):
    T, D = x.shape
    tr = _pick(T, 512)
    return pl.pallas_call(
        _rms_body, out_shape=jax.ShapeDtypeStruct((T, D), out_dtype), grid=(T // tr,),
        in_specs=[pl.BlockSpec((tr, D), lambda i: (i, 0)), pl.BlockSpec((1, D), lambda i: (0, 0))],
        out_specs=pl.BlockSpec((tr, D), lambda i: (i, 0)),
        compiler_params=_params(1, 32), name="rmsnorm")(x, g.reshape(1, D))


def _rms2_body(x_ref, g1_ref, g2_ref, o1_ref, o2_ref):
    x = x_ref[...]
    xn = x * lax.rsqrt(jnp.mean(x * x, axis=-1, keepdims=True) + EPS)
    o1_ref[...] = (xn * g1_ref[...]).astype(o1_ref.dtype)
    o2_ref[...] = (xn * g2_ref[...]).astype(o2_ref.dtype)


def _rmsnorm2(x, g1, g2, out_dtype):
    T, D = x.shape
    tr = _pick(T, 512)
    rows = pl.BlockSpec((tr, D), lambda i: (i, 0))
    gain = pl.BlockSpec((1, D), lambda i: (0, 0))
    return pl.pallas_call(
        _rms2_body, out_shape=(jax.ShapeDtypeStruct((T, D), out_dtype),) * 2, grid=(T // tr,),
        in_specs=[rows, gain, gain], out_specs=(rows, rows),
        compiler_params=_params(1, 32), name="rmsnorm2")(x, g1.reshape(1, D), g2.reshape(1, D))


def _rms_split_body(x_ref, g_ref, o1_ref, o2_ref, *, n_first):
    i = pl.program_id(0)
    y = _rms_rows(x_ref[...], g_ref[...])

    @pl.when(i < n_first)
    def _():
        o1_ref[...] = y

    @pl.when(i >= n_first)
    def _():
        o2_ref[...] = y


def _rmsnorm_split(x, g, rows_first):
    T, D = x.shape
    rest = T - rows_first
    tr = _pick(rows_first, 512)
    while rest % tr:
        tr //= 2
    n_first = rows_first // tr
    return pl.pallas_call(
        functools.partial(_rms_split_body, n_first=n_first),
        out_shape=(jax.ShapeDtypeStruct((rows_first, D), F32), jax.ShapeDtypeStruct((rest, D), F32)), grid=(T // tr,),
        in_specs=[pl.BlockSpec((tr, D), lambda i: (i, 0)), pl.BlockSpec((1, D), lambda i: (0, 0))],
        out_specs=(pl.BlockSpec((tr, D), lambda i: (jnp.minimum(i, n_first - 1), 0)),
                   pl.BlockSpec((tr, D), lambda i: (jnp.maximum(i - n_first, 0), 0))),
        compiler_params=_params(1, 32), name="rmsnorm_final")(x, g.reshape(1, D))


def _mm_plain_body(x_ref, w_ref, o_ref):
    o_ref[...] = _dot(x_ref[...], w_ref[...]).astype(o_ref.dtype)


def _mm_res_body(x_ref, w_ref, r_ref, o_ref):
    o_ref[...] = r_ref[...] + _dot(x_ref[...], w_ref[...])


def _mm_swiglu_body(x_ref, wg_ref, wu_ref, o_ref):
    x = x_ref[...]
    g = _dot(x, wg_ref[...])
    o_ref[...] = (g * _sigmoid(g) * _dot(x, wu_ref[...])).astype(o_ref.dtype)


def _mm_ple_body(h_ref, wg_ref, p_ref, wp_ref, r_ref, o_ref):
    gate = _sigmoid(_dot(h_ref[...], wg_ref[...]))
    o_ref[...] = r_ref[...] + gate * _dot(p_ref[...], wp_ref[...])


def _mm_rope_body(x_ref, w_ref, cos_ref, sin_ref, o_ref, *, n_rope, scale):
    acc = _dot(x_ref[...], w_ref[...])
    tm, tn = acc.shape
    cos = cos_ref[...]
    sin = sin_ref[...]
    lane = lax.broadcasted_iota(I32, (tm, LANES), 1)
    first_half = (lane % HEAD_DIM) < (HEAD_DIM // 2)
    for c in range(tn // LANES):
        x = acc[:, c * LANES:(c + 1) * LANES]
        if c < n_rope:
            partner = jnp.where(first_half, pltpu.roll(x, LANES - HEAD_DIM // 2, 1),
                                pltpu.roll(x, HEAD_DIM // 2, 1))
            x = (x * cos + partner * sin) * scale
        o_ref[:, c * LANES:(c + 1) * LANES] = x.astype(o_ref.dtype)


def _with_weight_cast(body, w_slots, n_in):
    def wrapped(*refs):
        ins, out, scratch = list(refs[:n_in]), refs[n_in], refs[n_in + 1:]

        @pl.when(pl.program_id(1) == 0)
        def _():
            for k, s in zip(w_slots, scratch):
                s[...] = ins[k][...].astype(BF16)

        for k, s in zip(w_slots, scratch):
            ins[k] = s
        body(*ins, out)
    return wrapped


def _mm_call(body, T, N, tm, tn, operands, out_dtype, vmem_mib, name):
    in_specs, args, w_slots, scratch = [], [], [], []
    for op in operands:
        kind, a = op[0], op[1]
        if kind == "lhs":
            in_specs.append(pl.BlockSpec((tm, a.shape[1]), lambda j, i: (i, 0)))
        elif kind == "w":
            in_specs.append(pl.BlockSpec((a.shape[0], tn), functools.partial(lambda j, i, off: (0, j + off), off=op[2])))
            w_slots.append(len(args))
            scratch.append(pltpu.VMEM((a.shape[0], tn), BF16))
        elif op[2]:
            in_specs.append(pl.BlockSpec((tm, tn), lambda j, i: (i, j)))
        else:
            in_specs.append(pl.BlockSpec((tm, a.shape[1]), lambda j, i: (i, 0)))
        args.append(a)
    return pl.pallas_call(
        _with_weight_cast(body, w_slots, len(args)), out_shape=jax.ShapeDtypeStruct((T, N), out_dtype),
        grid=(N // tn, T // tm), in_specs=in_specs, out_specs=pl.BlockSpec((tm, tn), lambda j, i: (i, j)),
        scratch_shapes=scratch, compiler_params=_params(2, vmem_mib), name=name)(*args)


def _ssd_body(*refs, Q, nc, has_init):
    if has_init:
        (z_ref, xbc_ref, dt_ref, cw_ref, cb_ref, dtb_ref, alog_ref, dfull_ref, nw_ref, conv0_ref, ssm0_ref,
         _, g_ref, convo_ref, ssmo_ref, xp, HT, ysc, act, xsb) = refs
    else:
        (z_ref, xbc_ref, dt_ref, cw_ref, cb_ref, dtb_ref, alog_ref, dfull_ref, nw_ref,
         g_ref, convo_ref, ssmo_ref, xp, HT, ysc, act, xsb) = refs
    c = pl.program_id(1)
    HPG = SSM_HEADS // SSM_GROUPS
    eye_n_bf = _eye(SSM_STATE, BF16)

    @pl.when(c == 0)
    def _init():
        xp[0:8, :] = jnp.zeros((8, CONV_DIM), F32)
        if has_init:
            xp[8 - (CONV_WIDTH - 1):8, :] = conv0_ref[...]
            for g in range(SSM_GROUPS):
                HT[g] = ssm0_ref[g * HPG:(g + 1) * HPG].reshape(HPG * SSM_HEAD_DIM, SSM_STATE).T
        else:
            HT[...] = jnp.zeros(HT.shape, F32)

    xp[8:8 + Q, :] = xbc_ref[...]
    for k in range(CONV_DIM // CONV_COLS):
        cs = slice(k * CONV_COLS, (k + 1) * CONV_COLS)
        xe = xp[:, cs]
        cw = cw_ref[:, cs]
        conv = cb_ref[:, cs]
        for t in range(CONV_WIDTH - 1):
            conv = conv + pltpu.roll(xe, CONV_WIDTH - 1 - t, 0)[8:, :] * cw[t:t + 1, :]
        conv = conv + xe[8:, :] * cw[CONV_WIDTH - 1:CONV_WIDTH, :]
        a = conv * _sigmoid(conv)
        act[:, cs] = a
        if k * CONV_COLS < D_INNER:
            xsb[:, cs] = a.astype(BF16)

    @pl.when(c == nc - 1)
    def _conv_out():
        convo_ref[...] = xp[8 + Q - (CONV_WIDTH - 1):8 + Q, :]

    xp[0:8, :] = xp[Q:Q + 8, :]

    dt = _softplus(dt_ref[...] + dtb_ref[...])
    dA = dt * (-jnp.exp(alog_ref[...]))
    tril = lax.broadcasted_iota(I32, (Q, Q), 0) >= lax.broadcasted_iota(I32, (Q, Q), 1)
    a_cs = jnp.dot(tril.astype(F32), dA, precision=HI, preferred_element_type=F32)
    a_last = a_cs[Q - 1:Q, :]
    w_end = jnp.exp(a_last - a_cs) * dt
    cdec = jnp.exp(a_last)

    n_pairs = SSM_HEADS // 2
    pr = lax.broadcasted_iota(I32, (n_pairs, SSM_HEADS), 0)
    pc_ = lax.broadcasted_iota(I32, (n_pairs, SSM_HEADS), 1)
    sel_a = (pc_ == 2 * pr).astype(F32)
    sel_b = (pc_ == 2 * pr + 1).astype(F32)

    def pair_rows(m):
        return jnp.concatenate([_dot_nt(sel_a, m, HI), _dot_nt(sel_b, m, HI)], axis=1)

    a_csT2, dtT2, w_endT2 = pair_rows(a_cs), pair_rows(dt), pair_rows(w_end)
    tok_r = lax.broadcasted_iota(I32, (Q, 2 * Q), 0)
    tok_c = lax.broadcasted_iota(I32, (Q, 2 * Q), 1)
    first_tok = tok_c < Q
    tril2 = tok_r >= jnp.where(first_tok, tok_c, tok_c - Q)
    first_ch = lax.broadcasted_iota(I32, (Q, PAIR_W), 1) < SSM_HEAD_DIM
    first_ch2 = lax.broadcasted_iota(I32, (2 * Q, PAIR_W), 1) < SSM_HEAD_DIM
    first_row = lax.broadcasted_iota(I32, (2 * Q, PAIR_W), 0) < Q
    ssq = jnp.zeros((Q, PAIR_W), F32)

    for g in range(SSM_GROUPS):
        b0 = D_INNER + g * SSM_STATE
        c0 = D_INNER + SSM_GROUPS * SSM_STATE + g * SSM_STATE
        Bg = act[:, b0:b0 + SSM_STATE].astype(BF16)
        Cg = act[:, c0:c0 + SSM_STATE].astype(BF16)
        cb = _dot_nt(Cg, Bg)
        cb2 = jnp.concatenate([cb, cb], axis=1)
        BT = _dot_nt(eye_n_bf, Bg)
        BT2 = jnp.concatenate([BT, BT], axis=1)
        Hg = HT[g]
        yoff = _dot(Cg, Hg.astype(BF16))
        for pp in range(HPG // 2):
            i = g * (HPG // 2) + pp
            ha, hb = 2 * i, 2 * i + 1
            pc = slice(i * PAIR_W, (i + 1) * PAIR_W)
            gc = slice(pp * PAIR_W, (pp + 1) * PAIR_W)
            col_a, col_b = a_cs[:, ha:ha + 1], a_cs[:, hb:hb + 1]
            seg = jnp.where(first_tok, col_a, col_b) - a_csT2[i:i + 1, :]
            dec = jnp.where(tril2, jnp.exp(seg), 0.0)
            M2 = (cb2 * dec * dtT2[i:i + 1, :]).astype(BF16)
            BTw2 = (BT2 * w_endT2[i:i + 1, :]).astype(BF16)
            x2 = xsb[:, pc]
            x2 = jnp.concatenate([x2, x2], axis=0)
            xbd = jnp.where(first_row == first_ch2, x2, jnp.zeros_like(x2))
            r = _dot(jnp.concatenate([M2, BTw2], axis=0), xbd)
            ea2 = jnp.exp(jnp.where(first_ch, col_a, col_b))
            y = r[:Q] + yoff[:, gc] * ea2 + act[:, pc] * dfull_ref[:, pc]
            zf = z_ref[:, pc].astype(F32)
            y = y * (zf * _sigmoid(zf))
            ysc[:, pc] = y
            ssq = ssq + y * y
            cd2 = jnp.where(first_ch[0:1, :], cdec[:, ha:ha + 1], cdec[:, hb:hb + 1])
            HT[g, :, gc] = Hg[:, gc] * cd2 + r[Q:]

    inv = lax.rsqrt(jnp.sum(ssq, axis=-1, keepdims=True) * (1.0 / D_INNER) + EPS)
    g_ref[...] = (ysc[...] * inv * nw_ref[...]).astype(g_ref.dtype)

    @pl.when(c == nc - 1)
    def _state_out():
        for g in range(SSM_GROUPS):
            ssmo_ref[g * HPG:(g + 1) * HPG] = HT[g].T.reshape(HPG, SSM_HEAD_DIM, SSM_STATE)


def _ssd_call(z, xbc, dt, prm, *, n_seq, seq_len, row0, Q, conv0=None, ssm0=None, g_in=None):
    T = z.shape[0]
    nc = seq_len // Q
    blk0 = row0 // Q
    has_init = conv0 is not None
    rows = lambda b, c: (blk0 + b * nc + c, 0)
    const = lambda b, c: (0, 0)
    in_specs = [pl.BlockSpec((Q, D_INNER), rows), pl.BlockSpec((Q, CONV_DIM), rows), pl.BlockSpec((Q, SSM_HEADS), rows),
                pl.BlockSpec((CONV_WIDTH, CONV_DIM), const), pl.BlockSpec((1, CONV_DIM), const),
                pl.BlockSpec((1, SSM_HEADS), const), pl.BlockSpec((1, SSM_HEADS), const),
                pl.BlockSpec((1, D_INNER), const), pl.BlockSpec((1, D_INNER), const)]
    args = [z, xbc, dt, prm["conv_w"], prm["conv_b"], prm["dt_bias"], prm["a_log"], prm["d_full"], prm["norm_w"]]
    aliases = {}
    if has_init:
        in_specs += [pl.BlockSpec((None, CONV_WIDTH - 1, CONV_DIM), lambda b, c: (b, 0, 0)),
                     pl.BlockSpec((None, SSM_HEADS, SSM_HEAD_DIM, SSM_STATE), lambda b, c: (b, 0, 0, 0)),
                     pl.BlockSpec(memory_space=pl.ANY)]
        args += [conv0, ssm0, g_in]
        aliases = {len(args) - 1: 0}
    out_shape = (jax.ShapeDtypeStruct((T, D_INNER), BF16),
                 jax.ShapeDtypeStruct((n_seq, CONV_WIDTH - 1, CONV_DIM), F32),
                 jax.ShapeDtypeStruct((n_seq, SSM_HEADS, SSM_HEAD_DIM, SSM_STATE), F32))
    out_specs = (pl.BlockSpec((Q, D_INNER), rows),
                 pl.BlockSpec((None, CONV_WIDTH - 1, CONV_DIM), lambda b, c: (b, 0, 0)),
                 pl.BlockSpec((None, SSM_HEADS, SSM_HEAD_DIM, SSM_STATE), lambda b, c: (b, 0, 0, 0)))
    scratch = [pltpu.VMEM((Q + 8, CONV_DIM), F32),
               pltpu.VMEM((SSM_GROUPS, SSM_STATE, D_INNER // SSM_GROUPS), F32),
               pltpu.VMEM((Q, D_INNER), F32), pltpu.VMEM((Q, CONV_DIM), F32), pltpu.VMEM((Q, D_INNER), BF16)]
    return pl.pallas_call(
        functools.partial(_ssd_body, Q=Q, nc=nc, has_init=has_init), out_shape=out_shape,
        grid=(n_seq, nc), in_specs=in_specs, out_specs=out_specs, scratch_shapes=scratch,
        input_output_aliases=aliases, compiler_params=_params(2, 40),
        name="ssd_sample" if has_init else "ssd_prompt")(*args)


def _attend(q_ref, sink_ref, o_ref, kband, vband, bias):
    ones = jnp.ones((kband.shape[1], HEAD_DIM), BF16)
    for h in range(N_Q_HEADS):
        kh = h // Q_PER_KV
        hd = slice(h * HEAD_DIM, (h + 1) * HEAD_DIM)
        s = _dot_nt(q_ref[:, hd], kband[kh])
        if bias is not None:
            s = s + bias
        sink = sink_ref[h]
        m = jnp.maximum(jnp.max(s, axis=-1, keepdims=True), sink)
        e = jnp.exp(s - m).astype(BF16)
        den = _dot(e, ones) + jnp.exp(sink - m)
        o_ref[:, hd] = (_dot(e, vband[kh]) / den).astype(o_ref.dtype)


def _stage_kv(kband, vband, row0, k_of, v_of):
    n = k_of(0).shape[0]
    for kh in range(N_KV_HEADS):
        kband[kh, row0:row0 + n, :] = k_of(kh).astype(BF16)
        vband[kh, row0:row0 + n, :] = v_of(kh).astype(BF16)


def _attn_prompt_body(sink_ref, q_ref, kv0_ref, kv1_ref, kv2_ref, o_ref, kband, vband):
    c = pl.program_id(1)
    KW = N_KV_HEADS * HEAD_DIM
    for j, ref in enumerate((kv0_ref, kv1_ref, kv2_ref)):
        _stage_kv(kband, vband, j * CHUNK,
                  functools.partial(lambda kh, r: r[:, kh * HEAD_DIM:(kh + 1) * HEAD_DIM], r=ref),
                  functools.partial(lambda kh, r: r[:, KW + kh * HEAD_DIM:KW + (kh + 1) * HEAD_DIM], r=ref))
    key = lax.broadcasted_iota(I32, (1, kband.shape[1]), 1)
    bias = jnp.where(key >= jnp.maximum(2 - c, 0) * CHUNK, 0.0, -jnp.inf).astype(F32)
    _attend(q_ref, sink_ref, o_ref, kband, vband, bias)


def _attn_sample_body(sink_ref, q_ref, ck_ref, cv_ref, kvn_ref, _, o_ref, kband, vband):
    KW = N_KV_HEADS * HEAD_DIM
    _stage_kv(kband, vband, 0, lambda kh: ck_ref[:, kh * HEAD_DIM:(kh + 1) * HEAD_DIM],
              lambda kh: cv_ref[:, kh * HEAD_DIM:(kh + 1) * HEAD_DIM])
    _stage_kv(kband, vband, WINDOW, lambda kh: kvn_ref[:, kh * HEAD_DIM:(kh + 1) * HEAD_DIM],
              lambda kh: kvn_ref[:, KW + kh * HEAD_DIM:KW + (kh + 1) * HEAD_DIM])
    _attend(q_ref, sink_ref, o_ref, kband, vband, None)


def _attention(q, kv, sinks, cache_k, cache_v, *, n_prompt, seq, n_dec, dec_len):
    T = q.shape[0]
    NQ = N_Q_HEADS * HEAD_DIM
    KW = N_KV_HEADS * HEAD_DIM
    nc = seq // CHUNK
    smem = pl.BlockSpec(memory_space=pltpu.SMEM)
    band = lambda back: (lambda b, c: (b * nc + jnp.maximum(c - back, 0), 0))
    bands = lambda keys: [pltpu.VMEM((N_KV_HEADS, keys, HEAD_DIM), BF16)] * 2
    o = pl.pallas_call(
        _attn_prompt_body, out_shape=jax.ShapeDtypeStruct((T, NQ), BF16), grid=(n_prompt, nc),
        in_specs=[smem, pl.BlockSpec((CHUNK, NQ), lambda b, c: (b * nc + c, 0)),
                  pl.BlockSpec((CHUNK, 2 * KW), band(2)), pl.BlockSpec((CHUNK, 2 * KW), band(1)),
                  pl.BlockSpec((CHUNK, 2 * KW), band(0))],
        out_specs=pl.BlockSpec((CHUNK, NQ), lambda b, c: (b * nc + c, 0)),
        scratch_shapes=bands(WINDOW + CHUNK),
        compiler_params=_params(2, 32), name="attn_prompt")(sinks, q, kv, kv, kv)
    blk0 = (n_prompt * seq) // dec_len
    return pl.pallas_call(
        _attn_sample_body, out_shape=jax.ShapeDtypeStruct((T, NQ), BF16), grid=(n_dec,),
        in_specs=[smem, pl.BlockSpec((dec_len, NQ), lambda b: (blk0 + b, 0)),
                  pl.BlockSpec((None, WINDOW, KW), lambda b: (b, 0, 0)),
                  pl.BlockSpec((None, WINDOW, KW), lambda b: (b, 0, 0)),
                  pl.BlockSpec((dec_len, 2 * KW), lambda b: (blk0 + b, 0)),
                  pl.BlockSpec(memory_space=pl.ANY)],
        out_specs=pl.BlockSpec((dec_len, NQ), lambda b: (blk0 + b, 0)),
        scratch_shapes=bands(WINDOW + dec_len),
        input_output_aliases={5: 0}, compiler_params=_params(1, 32), name="attn_sample")(
            sinks, q, cache_k.reshape(n_dec, WINDOW, KW), cache_v.reshape(n_dec, WINDOW, KW), kv, o)


def _router_body(x_ref, g_ref, rt_ref, idx_ref, w_ref):
    h = _rms_rows(x_ref[...], g_ref[...])
    lt = _dot_nt(rt_ref[...], h, HI)
    ids = lax.broadcasted_iota(I32, lt.shape, 0)
    m1 = jnp.max(lt, axis=0, keepdims=True)
    i1 = jnp.min(jnp.where(lt == m1, ids, N_EXPERTS), axis=0, keepdims=True)
    rest = jnp.where(ids == i1, -jnp.inf, lt)
    m2 = jnp.max(rest, axis=0, keepdims=True)
    i2 = jnp.min(jnp.where(rest == m2, ids, N_EXPERTS), axis=0, keepdims=True)
    e2 = jnp.exp(m2 - m1)
    w1 = 1.0 / (1.0 + e2)
    idx_ref[...] = jnp.concatenate([i1, i2], axis=0)
    w_ref[...] = jnp.concatenate([w1, e2 * w1], axis=0)


def _router(x, g, router):
    T = x.shape[0]
    tm = _pick(T, 512)
    return pl.pallas_call(
        _router_body, out_shape=(jax.ShapeDtypeStruct((2, T), I32), jax.ShapeDtypeStruct((2, T), F32)),
        grid=(T // tm,),
        in_specs=[pl.BlockSpec((tm, D_MODEL), lambda i: (i, 0)), pl.BlockSpec((1, D_MODEL), lambda i: (0, 0)),
                  pl.BlockSpec((N_EXPERTS, D_MODEL), lambda i: (0, 0))],
        out_specs=(pl.BlockSpec((2, tm), lambda i: (0, i)), pl.BlockSpec((2, tm), lambda i: (0, i))),
        compiler_params=_params(1, 32), name="moe_router")(x, g.reshape(1, D_MODEL), router.T)


def _route_plan(idx, tm, n_tiles):
    T = idx.shape[1]
    e_flat = idx.reshape(-1)
    onehot = (e_flat[:, None] == jnp.arange(N_EXPERTS, dtype=I32)[None, :]).astype(I32)
    csum = jnp.cumsum(onehot, axis=0)
    counts = csum[-1]
    rank = jnp.sum(csum * onehot, axis=1) - 1
    padded = ((counts + tm - 1) // tm) * tm
    ends = jnp.cumsum(padded)
    pos = jnp.sum((ends - padded)[None, :] * onehot, axis=1) + rank
    tok = jnp.tile(jnp.arange(T, dtype=I32), 2)
    src = jnp.zeros((n_tiles * tm,), I32).at[pos].set(tok)
    tile_start = jnp.arange(n_tiles, dtype=I32) * tm
    valid = tile_start < ends[-1]
    probe = jnp.minimum(tile_start, ends[-1] - 1)
    expert = jnp.minimum(jnp.sum((ends[None, :] <= probe[:, None]).astype(I32), axis=1), N_EXPERTS - 1)
    first = jnp.concatenate([jnp.ones((1,), I32), (expert[1:] != expert[:-1]).astype(I32)])
    return pos.reshape(2, T).astype(I32), src, expert, valid.astype(I32), first


def _row_copy(table_hbm, row, buf, r, sem):
    return pltpu.make_async_copy(table_hbm.at[pl.ds(row, 1)], buf.at[pl.ds(r, 1)], sem)


def _start_rows(index_of, n_rows, table_hbm, buf, sem):
    def start(r, carry):
        _row_copy(table_hbm, index_of(r), buf, r, sem).start()
        return carry
    lax.fori_loop(0, n_rows, start, 0, unroll=DMA_UNROLL)


def _wait_rows(n_rows, table_hbm, buf, sem):
    def wait(r, carry):
        _row_copy(table_hbm, 0, buf, r, sem).wait()
        return carry
    lax.fori_loop(0, n_rows, wait, 0, unroll=DMA_UNROLL)


def _moe_gather_body(src_ref, nxt_ref, x_hbm, g_ref, o_ref, buf, sem):
    i = pl.program_id(0)
    tm = buf.shape[1]
    slot = i % 2

    @pl.when(i == 0)
    def _():
        _start_rows(lambda r: src_ref[0, 0, r], tm, x_hbm, buf.at[0], sem.at[0])

    @pl.when(i + 1 < pl.num_programs(0))
    def _():
        _start_rows(lambda r: nxt_ref[0, 0, r], tm, x_hbm, buf.at[1 - slot], sem.at[1 - slot])

    _wait_rows(tm, x_hbm, buf.at[slot], sem.at[slot])
    o_ref[...] = _rms_rows(buf[slot], g_ref[...]).astype(o_ref.dtype)


def _moe_gather(x, g, src, tm):
    n_tiles = src.shape[0] // tm
    src3 = src.reshape(n_tiles, 1, tm)
    return pl.pallas_call(
        _moe_gather_body, out_shape=jax.ShapeDtypeStruct((n_tiles * tm, D_MODEL), BF16), grid=(n_tiles,),
        in_specs=[pl.BlockSpec((1, 1, tm), lambda i: (i, 0, 0), memory_space=pltpu.SMEM),
                  pl.BlockSpec((1, 1, tm), lambda i: (jnp.minimum(i + 1, n_tiles - 1), 0, 0), memory_space=pltpu.SMEM),
                  pl.BlockSpec(memory_space=pl.ANY), pl.BlockSpec((1, D_MODEL), lambda i: (0, 0))],
        out_specs=pl.BlockSpec((tm, D_MODEL), lambda i: (i, 0)),
        scratch_shapes=[pltpu.VMEM((2, tm, D_MODEL), F32), pltpu.SemaphoreType.DMA((2,))],
        compiler_params=_params(1, 32), name="moe_gather")(src3, src3, x, g.reshape(1, D_MODEL))


def _moe_up_body(te_ref, tv_ref, tf_ref, x_ref, wg_ref, wu_ref, o_ref, wg_bf, wu_bf):
    i = pl.program_id(1)

    @pl.when(tf_ref[i] != 0)
    def _():
        wg_bf[...] = wg_ref[...].astype(BF16)
        wu_bf[...] = wu_ref[...].astype(BF16)

    @pl.when(tv_ref[i] != 0)
    def _():
        x = x_ref[...]
        g = _dot(x, wg_bf[...])
        o_ref[...] = (g * _sigmoid(g) * _dot(x, wu_bf[...])).astype(o_ref.dtype)

    @pl.when(tv_ref[i] == 0)
    def _():
        o_ref[...] = jnp.zeros(o_ref.shape, o_ref.dtype)


def _moe_down_body(te_ref, tv_ref, x_ref, w_ref, o_ref):
    i = pl.program_id(1)

    @pl.when(tv_ref[i] != 0)
    def _():
        o_ref[...] = _dot(x_ref[...], w_ref[...])

    @pl.when(tv_ref[i] == 0)
    def _():
        o_ref[...] = jnp.zeros(o_ref.shape, o_ref.dtype)


def _moe_experts(xs, w_gate_up, w_down, tile_expert, tile_valid, tile_first, tm):
    P = xs.shape[0]
    n_tiles = P // tm
    tn = _pick(EXPERT_FF, 512)
    up_blocks = EXPERT_FF // tn
    act = pl.pallas_call(
        _moe_up_body, out_shape=jax.ShapeDtypeStruct((P, EXPERT_FF), BF16),
        grid_spec=pltpu.PrefetchScalarGridSpec(
            num_scalar_prefetch=3, grid=(up_blocks, n_tiles),
            in_specs=[pl.BlockSpec((tm, D_MODEL), lambda j, i, te, tv, tf: (i, 0)),
                      pl.BlockSpec((None, D_MODEL, tn), lambda j, i, te, tv, tf: (te[i], 0, j)),
                      pl.BlockSpec((None, D_MODEL, tn), lambda j, i, te, tv, tf: (te[i], 0, j + up_blocks))],
            out_specs=pl.BlockSpec((tm, tn), lambda j, i, te, tv, tf: (i, j)),
            scratch_shapes=[pltpu.VMEM((D_MODEL, tn), BF16)] * 2),
        compiler_params=_params(2, 40), name="moe_up")(tile_expert, tile_valid, tile_first, xs, w_gate_up, w_gate_up)
    tn = _pick(D_MODEL, 512)
    return pl.pallas_call(
        _moe_down_body, out_shape=jax.ShapeDtypeStruct((P, D_MODEL), F32),
        grid_spec=pltpu.PrefetchScalarGridSpec(
            num_scalar_prefetch=2, grid=(D_MODEL // tn, n_tiles),
            in_specs=[pl.BlockSpec((tm, EXPERT_FF), lambda j, i, te, tv: (i, 0)),
                      pl.BlockSpec((None, EXPERT_FF, tn), lambda j, i, te, tv: (te[i], 0, j))],
            out_specs=pl.BlockSpec((tm, tn), lambda j, i, te, tv: (i, j))),
        compiler_params=_params(2, 48), name="moe_down")(tile_expert, tile_valid, act, w_down)


def _moe_combine_body(pos_ref, nxt_ref, x_ref, w_ref, ys_hbm, g_ref, xo_ref, ho_ref, buf, sem):
    i = pl.program_id(0)
    tc = buf.shape[2]
    slot = i % 2

    def start(table, s):
        for k in range(2):
            _start_rows(functools.partial(lambda r, k: table[0, k, r], k=k), tc, ys_hbm, buf.at[s, k], sem.at[s])

    @pl.when(i == 0)
    def _():
        start(pos_ref, 0)

    @pl.when(i + 1 < pl.num_programs(0))
    def _():
        start(nxt_ref, 1 - slot)

    for k in range(2):
        _wait_rows(tc, ys_hbm, buf.at[slot, k], sem.at[slot])
    w = w_ref[...]
    x = x_ref[...] + (w[:, 0:1] * buf[slot, 0] + w[:, 1:2] * buf[slot, 1])
    xo_ref[...] = x
    ho_ref[...] = _rms_rows(x, g_ref[...]).astype(ho_ref.dtype)


def _moe_combine(x, ys, pos, w, g):
    T = x.shape[0]
    tc = _pick(T, 256)
    n = T // tc
    pos_t = pos.reshape(2, n, tc).transpose(1, 0, 2)
    return pl.pallas_call(
        _moe_combine_body,
        out_shape=(jax.ShapeDtypeStruct((T, D_MODEL), F32), jax.ShapeDtypeStruct((T, D_MODEL), BF16)),
        grid=(n,),
        in_specs=[pl.BlockSpec((1, 2, tc), lambda i: (i, 0, 0), memory_space=pltpu.SMEM),
                  pl.BlockSpec((1, 2, tc), lambda i: (jnp.minimum(i + 1, n - 1), 0, 0), memory_space=pltpu.SMEM),
                  pl.BlockSpec((tc, D_MODEL), lambda i: (i, 0)), pl.BlockSpec((tc, 2), lambda i: (i, 0)),
                  pl.BlockSpec(memory_space=pl.ANY), pl.BlockSpec((1, D_MODEL), lambda i: (0, 0))],
        out_specs=(pl.BlockSpec((tc, D_MODEL), lambda i: (i, 0)), pl.BlockSpec((tc, D_MODEL), lambda i: (i, 0))),
        scratch_shapes=[pltpu.VMEM((2, 2, tc, D_MODEL), F32), pltpu.SemaphoreType.DMA((2,))],
        compiler_params=_params(1, 40), name="moe_combine")(pos_t, pos_t, x, w.T, ys, g.reshape(1, D_MODEL))


def _rope_tables(pos):
    half = HEAD_DIM // 2
    inv_freq = ROPE_THETA ** (-jnp.arange(half, dtype=F32) * 2.0 / HEAD_DIM)
    ang = pos.astype(F32)[:, None] * inv_freq[None, :]
    cos, sin = jnp.cos(ang), jnp.sin(ang)
    reps = LANES // HEAD_DIM
    return jnp.tile(jnp.concatenate([cos, cos], axis=-1), (1, reps)), jnp.tile(jnp.concatenate([-sin, sin], axis=-1), (1, reps))


def kernel(x_prompt, x_sample, p_prompt, p_sample, cache_conv, state_ssm, cache_k, cache_v, norm_mix, norm_ffn, norm_ple, norm_final, mamba_in_proj, mamba_conv_w, mamba_conv_b, mamba_dt_bias, mamba_a_log, mamba_d, mamba_norm, mamba_out_proj, norm_kv, w_kv, w_q, attn_sinks, w_o, ffn_w_gate_up, ffn_w_down, moe_router, moe_w_gate_up, moe_w_down, ple_proj, ple_gate):
    n_prompt, seq, _ = x_prompt.shape
    n_dec, dec_len, _ = x_sample.shape
    Tp = n_prompt * seq
    T = Tp + n_dec * dec_len
    assert seq % CHUNK == 0 and Tp % dec_len == 0 and dec_len % 8 == 0

    bf = lambda w: w.astype(BF16)
    x0 = jnp.concatenate([x_prompt.reshape(Tp, D_MODEL), x_sample.reshape(-1, D_MODEL)], axis=0)
    p_all = jnp.concatenate([p_prompt.reshape(2, Tp, PLE_DIM), p_sample.reshape(2, -1, PLE_DIM)], axis=1).astype(BF16)
    pos = jnp.concatenate([jnp.tile(jnp.arange(seq, dtype=I32), n_prompt),
                           PAST_LEN + jnp.tile(jnp.arange(dec_len, dtype=I32), n_dec)])
    cos_t, sin_t = _rope_tables(pos)

    tm_wide = _pick(T, 1536)
    tm_mid = _pick(T, 768)
    tm_deep = _pick(T, 512)

    w_in = mamba_in_proj[0]
    h = _rmsnorm(x0, norm_mix[0], BF16)
    tn = _pick(D_INNER, 1024)
    z = _mm_call(_mm_plain_body, T, D_INNER, tm_wide, tn, [("lhs", h), ("w", w_in, 0)], BF16, 56, "in_proj_z")
    xbc = _mm_call(_mm_plain_body, T, CONV_DIM, tm_wide, tn, [("lhs", h), ("w", w_in, D_INNER // tn)], F32, 56, "in_proj_xbc")
    dt = _mm_call(_mm_plain_body, T, SSM_HEADS, tm_wide, SSM_HEADS,
                  [("lhs", h), ("w", w_in[:, D_INNER + CONV_DIM:], 0)], F32, 32, "in_proj_dt")
    ssd_prm = dict(conv_w=mamba_conv_w[0], conv_b=mamba_conv_b[0].reshape(1, CONV_DIM),
                   dt_bias=mamba_dt_bias[0].reshape(1, SSM_HEADS), a_log=mamba_a_log[0].reshape(1, SSM_HEADS),
                   d_full=jnp.repeat(mamba_d[0], SSM_HEAD_DIM).reshape(1, D_INNER),
                   norm_w=mamba_norm[0].reshape(1, D_INNER))
    g, conv_prompt, ssm_prompt = _ssd_call(z, xbc, dt, ssd_prm, n_seq=n_prompt, seq_len=seq, row0=0, Q=CHUNK)
    g, conv_sample, ssm_sample = _ssd_call(z, xbc, dt, ssd_prm, n_seq=n_dec, seq_len=dec_len, row0=Tp, Q=dec_len,
                                           conv0=cache_conv[0], ssm0=state_ssm[0], g_in=g)
    tn = _pick(D_MODEL, 512)
    x1 = _mm_call(_mm_res_body, T, D_MODEL, tm_mid, tn,
                  [("lhs", g), ("w", mamba_out_proj[0], 0), ("row", x0, True)], F32, 48, "out_proj")

    h = _rmsnorm(x1, norm_ffn[0], BF16)
    tn = _pick(D_FF, 512)
    w_gu = ffn_w_gate_up[0]
    a = _mm_call(_mm_swiglu_body, T, D_FF, tm_wide, tn,
                 [("lhs", h), ("w", w_gu, 0), ("w", w_gu, D_FF // tn)], BF16, 56, "ffn_up")
    tn = _pick(D_MODEL, 512)
    x2 = _mm_call(_mm_res_body, T, D_MODEL, tm_deep, tn,
                  [("lhs", a), ("w", ffn_w_down[0], 0), ("row", x1, True)], F32, 52, "ffn_down")
    h = _rmsnorm(x2, norm_ple[0], BF16)
    tn = _pick(D_MODEL, 512)
    x3 = _mm_call(_mm_ple_body, T, D_MODEL, tm_wide, tn,
                  [("lhs", h), ("w", ple_gate[0], 0), ("lhs", p_all[0]), ("w", ple_proj[0], 0), ("row", x2, True)],
                  F32, 48, "ple0")

    KW = N_KV_HEADS * HEAD_DIM
    hk, hq = _rmsnorm2(x3, norm_kv, norm_mix[1], BF16)
    kv = _mm_call(functools.partial(_mm_rope_body, n_rope=KW // LANES, scale=1.0), T, 2 * KW, tm_wide, 2 * KW,
                  [("lhs", hk), ("w", w_kv, 0), ("row", cos_t, False), ("row", sin_t, False)], F32, 48, "kv_proj")
    tn = _pick(D_MODEL, 512)
    q = _mm_call(functools.partial(_mm_rope_body, n_rope=tn // LANES, scale=HEAD_DIM ** -0.5), T, D_MODEL, tm_wide, tn,
                 [("lhs", hq), ("w", w_q[0], 0), ("row", cos_t, False), ("row", sin_t, False)], BF16, 48, "q_proj")
    o = _attention(q, kv, attn_sinks[0], cache_k, cache_v, n_prompt=n_prompt, seq=seq, n_dec=n_dec, dec_len=dec_len)
    tn = _pick(D_MODEL, 1024)
    x4 = _mm_call(_mm_res_body, T, D_MODEL, tm_mid, tn,
                  [("lhs", o), ("w", w_o[0], 0), ("row", x3, True)], F32, 48, "o_proj")

    tm_moe = _pick(T, 512)
    n_tiles = (2 * T) // tm_moe + N_EXPERTS
    idx, gate_w = _router(x4, norm_ffn[1], moe_router[0])
    pos_sorted, src, tile_expert, tile_valid, tile_first = _route_plan(idx, tm_moe, n_tiles)
    xs = _moe_gather(x4, norm_ffn[1], src, tm_moe)
    ys = _moe_experts(xs, moe_w_gate_up[0], bf(moe_w_down[0]), tile_expert, tile_valid, tile_first, tm_moe)
    x5, h = _moe_combine(x4, ys, pos_sorted, gate_w, norm_ple[1])
    tn = _pick(D_MODEL, 512)
    x6 = _mm_call(_mm_ple_body, T, D_MODEL, tm_wide, tn,
                  [("lhs", h), ("w", ple_gate[1], 0), ("lhs", p_all[1]), ("w", ple_proj[1], 0), ("row", x5, True)],
                  F32, 48, "ple1")
    y_prompt, y_sample = _rmsnorm_split(x6, norm_final, Tp)

    k_all = kv[:, :KW]
    v_all = kv[:, KW:]
    tail = lambda a: a[:Tp].reshape(n_prompt, seq, N_KV_HEADS, HEAD_DIM)[:, seq - WINDOW:]
    new = lambda a: a[Tp:].reshape(n_dec, dec_len, N_KV_HEADS, HEAD_DIM)
    k_sample = jnp.concatenate([cache_k, new(k_all)], axis=1)[:, -WINDOW:]
    v_sample = jnp.concatenate([cache_v, new(v_all)], axis=1)[:, -WINDOW:]
    return (y_prompt.reshape(n_prompt, seq, D_MODEL), y_sample.reshape(n_dec, dec_len, D_MODEL),
            conv_prompt[None], ssm_prompt[None], tail(k_all), tail(v_all),
            conv_sample[None], ssm_sample[None], k_sample, v_sample)
```

```python
import functools

import jax
import jax.numpy as jnp
from jax import lax
from jax.experimental import pallas as pl
from jax.experimental.pallas import tpu as pltpu

F32 = jnp.float32
BF16 = jnp.bfloat16
I32 = jnp.int32
HI = lax.Precision.HIGHEST

D_MODEL = 2048
D_INNER = 4096
SSM_HEADS = 64
SSM_HEAD_DIM = 64
SSM_GROUPS = 8
SSM_STATE = 128
CONV_WIDTH = 4
CONV_DIM = D_INNER + 2 * SSM_GROUPS * SSM_STATE
N_Q_HEADS = 32
N_KV_HEADS = 4
Q_PER_KV = N_Q_HEADS // N_KV_HEADS
HEAD_DIM = 64
WINDOW = 128
CHUNK = 64
ROPE_THETA = 10000.0
D_FF = 5632
N_EXPERTS = 8
EXPERT_FF = 7168
PLE_DIM = 256
PAST_LEN = 4096
EPS = 1e-6

LANES = 128
MIB = 1024 * 1024
NT_DIMS = (((1,), (1,)), ((), ()))
DMA_UNROLL = 8
CONV_COLS = 512
PAIR_W = 2 * SSM_HEAD_DIM
MOE_ROW_SPLIT = 4


def _params(n_axes, vmem_mib):
    return pltpu.CompilerParams(dimension_semantics=("arbitrary",) * n_axes,
                                vmem_limit_bytes=vmem_mib * MIB)


def _pick(n, pref):
    for t in (1536, 1024, 768, 512, 384, 256, 128, 96, 64, 32, 16, 8):
        if t <= pref and n % t == 0:
            return t
    raise ValueError(f"no tile for {n}")


def _dot(a, b):
    return jnp.dot(a, b, preferred_element_type=F32)


def _dot_nt(a, b, precision=None):
    return lax.dot_general(a, b, NT_DIMS, precision=precision, preferred_element_type=F32)


def _sigmoid(x):
    return 1.0 / (1.0 + jnp.exp(-x))


def _softplus(x):
    return jnp.maximum(x, 0.0) + jnp.log1p(jnp.exp(-jnp.abs(x)))


def _eye(n, dtype):
    r = lax.broadcasted_iota(I32, (n, n), 0)
    c = lax.broadcasted_iota(I32, (n, n), 1)
    return (r == c).astype(dtype)


def _rms_rows(x, g):
    inv = lax.rsqrt(jnp.mean(x * x, axis=-1, keepdims=True) + EPS)
    return x * inv * g


def _rms_body(x_ref, g_ref, o_ref):
    o_ref[...] = _rms_rows(x_ref[...], g_ref[...]).astype(o_ref.dtype)


def _rmsnorm(x, g, out_dtype):
    T, D = x.shape
    tr = _pick(T, 512)
    return pl.pallas_call(
        _rms_body, out_shape=jax.ShapeDtypeStruct((T, D), out_dtype), grid=(T // tr,),
        in_specs=[pl.BlockSpec((tr, D), lambda i: (i, 0)), pl.BlockSpec((1, D), lambda i: (0, 0))],
        out_specs=pl.BlockSpec((tr, D), lambda i: (i, 0)),
        compiler_params=_params(1, 32), name="rmsnorm")(x, g.reshape(1, D))


def _rms2_body(x_ref, g1_ref, g2_ref, o1_ref, o2_ref):
    x = x_ref[...]
    xn = x * lax.rsqrt(jnp.mean(x * x, axis=-1, keepdims=True) + EPS)
    o1_ref[...] = (xn * g1_ref[...]).astype(o1_ref.dtype)
    o2_ref[...] = (xn * g2_ref[...]).astype(o2_ref.dtype)


def _rmsnorm2(x, g1, g2, out_dtype):
    T, D = x.shape
    tr = _pick(T, 512)
    rows = pl.BlockSpec((tr, D), lambda i: (i, 0))
    gain = pl.BlockSpec((1, D), lambda i: (0, 0))
    return pl.pallas_call(
        _rms2_body, out_shape=(jax.ShapeDtypeStruct((T, D), out_dtype),) * 2, grid=(T // tr,),
        in_specs=[rows, gain, gain], out_specs=(rows, rows),
        compiler_params=_params(1, 32), name="rmsnorm2")(x, g1.reshape(1, D), g2.reshape(1, D))


def _rms_split_body(x_ref, g_ref, o1_ref, o2_ref, *, n_first):
    i = pl.program_id(0)
    y = _rms_rows(x_ref[...], g_ref[...])

    @pl.when(i < n_first)
    def _():
        o1_ref[...] = y

    @pl.when(i >= n_first)
    def _():
        o2_ref[...] = y


def _rmsnorm_split(x, g, rows_first):
    T, D = x.shape
    rest = T - rows_first
    tr = _pick(rows_first, 512)
    while rest % tr:
        tr //= 2
    n_first = rows_first // tr
    return pl.pallas_call(
        functools.partial(_rms_split_body, n_first=n_first),
        out_shape=(jax.ShapeDtypeStruct((rows_first, D), F32), jax.ShapeDtypeStruct((rest, D), F32)), grid=(T // tr,),
        in_specs=[pl.BlockSpec((tr, D), lambda i: (i, 0)), pl.BlockSpec((1, D), lambda i: (0, 0))],
        out_specs=(pl.BlockSpec((tr, D), lambda i: (jnp.minimum(i, n_first - 1), 0)),
                   pl.BlockSpec((tr, D), lambda i: (jnp.maximum(i - n_first, 0), 0))),
        compiler_params=_params(1, 32), name="rmsnorm_final")(x, g.reshape(1, D))


def _mm_plain_body(x_ref, w_ref, o_ref):
    o_ref[...] = _dot(x_ref[...], w_ref[...]).astype(o_ref.dtype)


def _mm_res_body(x_ref, w_ref, r_ref, o_ref):
    o_ref[...] = r_ref[...] + _dot(x_ref[...], w_ref[...])


def _mm_swiglu_body(x_ref, wg_ref, wu_ref, o_ref):
    x = x_ref[...]
    g = _dot(x, wg_ref[...])
    o_ref[...] = (g * _sigmoid(g) * _dot(x, wu_ref[...])).astype(o_ref.dtype)


def _mm_ple_body(h_ref, wg_ref, p_ref, wp_ref, r_ref, o_ref):
    gate = _sigmoid(_dot(h_ref[...], wg_ref[...]))
    o_ref[...] = r_ref[...] + gate * _dot(p_ref[...], wp_ref[...])


def _mm_rope_body(x_ref, w_ref, cos_ref, sin_ref, o_ref, *, n_rope, scale):
    acc = _dot(x_ref[...], w_ref[...])
    tm, tn = acc.shape
    cos = cos_ref[...]
    sin = sin_ref[...]
    lane = lax.broadcasted_iota(I32, (tm, LANES), 1)
    first_half = (lane % HEAD_DIM) < (HEAD_DIM // 2)
    for c in range(tn // LANES):
        x = acc[:, c * LANES:(c + 1) * LANES]
        if c < n_rope:
            partner = jnp.where(first_half, pltpu.roll(x, LANES - HEAD_DIM // 2, 1),
                                pltpu.roll(x, HEAD_DIM // 2, 1))
            x = (x * cos + partner * sin) * scale
        o_ref[:, c * LANES:(c + 1) * LANES] = x.astype(o_ref.dtype)


def _with_weight_cast(body, w_slots, n_in):
    def wrapped(*refs):
        ins, out, scratch = list(refs[:n_in]), refs[n_in], refs[n_in + 1:]

        @pl.when(pl.program_id(1) == 0)
        def _():
            for k, s in zip(w_slots, scratch):
                s[...] = ins[k][...].astype(BF16)

        for k, s in zip(w_slots, scratch):
            ins[k] = s
        body(*ins, out)
    return wrapped


def _mm_call(body, T, N, tm, tn, operands, out_dtype, vmem_mib, name):
    in_specs, args, w_slots, scratch = [], [], [], []
    for op in operands:
        kind, a = op[0], op[1]
        if kind == "lhs":
            in_specs.append(pl.BlockSpec((tm, a.shape[1]), lambda j, i: (i, 0)))
        elif kind == "w":
            in_specs.append(pl.BlockSpec((a.shape[0], tn), functools.partial(lambda j, i, off: (0, j + off), off=op[2])))
            w_slots.append(len(args))
            scratch.append(pltpu.VMEM((a.shape[0], tn), BF16))
        elif op[2]:
            in_specs.append(pl.BlockSpec((tm, tn), lambda j, i: (i, j)))
        else:
            in_specs.append(pl.BlockSpec((tm, a.shape[1]), lambda j, i: (i, 0)))
        args.append(a)
    return pl.pallas_call(
        _with_weight_cast(body, w_slots, len(args)), out_shape=jax.ShapeDtypeStruct((T, N), out_dtype),
        grid=(N // tn, T // tm), in_specs=in_specs, out_specs=pl.BlockSpec((tm, tn), lambda j, i: (i, j)),
        scratch_shapes=scratch, compiler_params=_params(2, vmem_mib), name=name)(*args)


def _ssd_body(*refs, Q, nc, has_init):
    if has_init:
        (z_ref, xbc_ref, dt_ref, cw_ref, cb_ref, dtb_ref, alog_ref, dfull_ref, nw_ref, conv0_ref, ssm0_ref,
         _, g_ref, convo_ref, ssmo_ref, xp, HT, ysc, act, xsb) = refs
    else:
        (z_ref, xbc_ref, dt_ref, cw_ref, cb_ref, dtb_ref, alog_ref, dfull_ref, nw_ref,
         g_ref, convo_ref, ssmo_ref, xp, HT, ysc, act, xsb) = refs
    c = pl.program_id(1)
    HPG = SSM_HEADS // SSM_GROUPS
    eye_n_bf = _eye(SSM_STATE, BF16)

    @pl.when(c == 0)
    def _init():
        xp[0:8, :] = jnp.zeros((8, CONV_DIM), F32)
        if has_init:
            xp[8 - (CONV_WIDTH - 1):8, :] = conv0_ref[...]
            for g in range(SSM_GROUPS):
                HT[g] = ssm0_ref[g * HPG:(g + 1) * HPG].reshape(HPG * SSM_HEAD_DIM, SSM_STATE).T
        else:
            HT[...] = jnp.zeros(HT.shape, F32)

    xp[8:8 + Q, :] = xbc_ref[...]
    for k in range(CONV_DIM // CONV_COLS):
        cs = slice(k * CONV_COLS, (k + 1) * CONV_COLS)
        xe = xp[:, cs]
        cw = cw_ref[:, cs]
        conv = cb_ref[:, cs]
        for t in range(CONV_WIDTH - 1):
            conv = conv + pltpu.roll(xe, CONV_WIDTH - 1 - t, 0)[8:, :] * cw[t:t + 1, :]
        conv = conv + xe[8:, :] * cw[CONV_WIDTH - 1:CONV_WIDTH, :]
        a = conv * _sigmoid(conv)
        act[:, cs] = a
        if k * CONV_COLS < D_INNER:
            xsb[:, cs] = a.astype(BF16)

    @pl.when(c == nc - 1)
    def _conv_out():
        convo_ref[...] = xp[8 + Q - (CONV_WIDTH - 1):8 + Q, :]

    xp[0:8, :] = xp[Q:Q + 8, :]

    dt = _softplus(dt_ref[...] + dtb_ref[...])
    dA = dt * (-jnp.exp(alog_ref[...]))
    tril = lax.broadcasted_iota(I32, (Q, Q), 0) >= lax.broadcasted_iota(I32, (Q, Q), 1)
    a_cs = jnp.dot(tril.astype(F32), dA, precision=HI, preferred_element_type=F32)
    a_last = a_cs[Q - 1:Q, :]
    w_end = jnp.exp(a_last - a_cs) * dt
    cdec = jnp.exp(a_last)

    n_pairs = SSM_HEADS // 2
    pr = lax.broadcasted_iota(I32, (n_pairs, SSM_HEADS), 0)
    pc_ = lax.broadcasted_iota(I32, (n_pairs, SSM_HEADS), 1)
    sel_a = (pc_ == 2 * pr).astype(F32)
    sel_b = (pc_ == 2 * pr + 1).astype(F32)

    def pair_rows(m):
        return jnp.concatenate([_dot_nt(sel_a, m, HI), _dot_nt(sel_b, m, HI)], axis=1)

    a_csT2, dtT2, w_endT2 = pair_rows(a_cs), pair_rows(dt), pair_rows(w_end)
    tok_r =lax.broadcasted_iota(I32, (Q, 2 * Q), 0)
    tok_c = lax.broadcasted_iota(I32, (Q, 2 * Q), 1)
    first_tok = tok_c < Q
    tril2 = tok_r >= jnp.where(first_tok, tok_c, tok_c - Q)
    first_ch = lax.broadcasted_iota(I32, (Q, PAIR_W), 1) < SSM_HEAD_DIM
    first_ch2 = lax.broadcasted_iota(I32, (2 * Q, PAIR_W), 1) < SSM_HEAD_DIM
    first_row = lax.broadcasted_iota(I32, (2 * Q, PAIR_W), 0) < Q
    ssq = jnp.zeros((Q, PAIR_W), F32)

    for g in range(SSM_GROUPS):
        b0 = D_INNER + g * SSM_STATE
        c0 = D_INNER + SSM_GROUPS * SSM_STATE + g * SSM_STATE
        Bg = act[:, b0:b0 + SSM_STATE].astype(BF16)
        Cg = act[:, c0:c0 + SSM_STATE].astype(BF16)
        cb = _dot_nt(Cg, Bg)
        cb2 = jnp.concatenate([cb, cb], axis=1)
        BT = _dot_nt(eye_n_bf, Bg)
        BT2 = jnp.concatenate([BT, BT], axis=1)
        Hg = HT[g]
        yoff = _dot(Cg, Hg.astype(BF16))
        for pp in range(HPG // 2):
            i = g * (HPG // 2) + pp
            ha, hb = 2 * i, 2 * i + 1
            pc = slice(i * PAIR_W, (i + 1) * PAIR_W)
            gc = slice(pp * PAIR_W, (pp + 1) * PAIR_W)
            col_a, col_b = a_cs[:, ha:ha + 1], a_cs[:, hb:hb + 1]
            seg = jnp.where(first_tok, col_a, col_b) - a_csT2[i:i + 1, :]
            dec = jnp.where(tril2, jnp.exp(seg), 0.0)
            M2 = (cb2 * dec * dtT2[i:i + 1, :]).astype(BF16)
            BTw2 = (BT2 * w_endT2[i:i + 1, :]).astype(BF16)
            x2 = xsb[:, pc]
            x2 = jnp.concatenate([x2, x2], axis=0)
            xbd = jnp.where(first_row == first_ch2, x2, jnp.zeros_like(x2))
            r = _dot(jnp.concatenate([M2, BTw2], axis=0), xbd)
            ea2 = jnp.exp(jnp.where(first_ch, col_a, col_b))
            y = r[:Q] + yoff[:, gc] * ea2 + act[:, pc] * dfull_ref[:, pc]
            zf = z_ref[:, pc].astype(F32)
            y = y * (zf * _sigmoid(zf))
            ysc[:, pc] = y
            ssq = ssq + y * y
            cd2 = jnp.where(first_ch[0:1, :], cdec[:, ha:ha + 1], cdec[:, hb:hb + 1])
            HT[g, :, gc] = Hg[:, gc] * cd2 + r[Q:]

    inv = lax.rsqrt(jnp.sum(ssq, axis=-1, keepdims=True) * (1.0 / D_INNER) + EPS)
    g_ref[...] = (ysc[...] * inv * nw_ref[...]).astype(g_ref.dtype)

    @pl.when(c == nc - 1)
    def _state_out():
        for g in range(SSM_GROUPS):
            ssmo_ref[g * HPG:(g + 1) * HPG] = HT[g].T.reshape(HPG, SSM_HEAD_DIM, SSM_STATE)


def _ssd_call(z, xbc, dt, prm, *, n_seq, seq_len, row0, Q, conv0=None, ssm0=None, g_in=None):
    T = z.shape[0]
    nc = seq_len // Q
    blk0 = row0 // Q
    has_init = conv0 is not None
    rows = lambda b, c: (blk0 + b * nc + c, 0)
    const = lambda b, c: (0, 0)
    in_specs = [pl.BlockSpec((Q, D_INNER), rows), pl.BlockSpec((Q, CONV_DIM), rows), pl.BlockSpec((Q, SSM_HEADS), rows),
                pl.BlockSpec((CONV_WIDTH, CONV_DIM), const), pl.BlockSpec((1, CONV_DIM), const),
                pl.BlockSpec((1, SSM_HEADS), const), pl.BlockSpec((1, SSM_HEADS), const),
                pl.BlockSpec((1, D_INNER), const), pl.BlockSpec((1, D_INNER), const)]
    args = [z, xbc, dt, prm["conv_w"], prm["conv_b"], prm["dt_bias"], prm["a_log"], prm["d_full"], prm["norm_w"]]
    aliases = {}
    if has_init:
        in_specs += [pl.BlockSpec((None, CONV_WIDTH - 1, CONV_DIM), lambda b, c: (b, 0, 0)),
                     pl.BlockSpec((None, SSM_HEADS, SSM_HEAD_DIM, SSM_STATE), lambda b, c: (b, 0, 0, 0)),
                     pl.BlockSpec(memory_space=pl.ANY)]
        args += [conv0, ssm0, g_in]
        aliases = {len(args) - 1: 0}
    out_shape = (jax.ShapeDtypeStruct((T, D_INNER), BF16),
                 jax.ShapeDtypeStruct((n_seq, CONV_WIDTH - 1, CONV_DIM), F32),
                 jax.ShapeDtypeStruct((n_seq, SSM_HEADS, SSM_HEAD_DIM, SSM_STATE), F32))
    out_specs = (pl.BlockSpec((Q, D_INNER), rows),
                 pl.BlockSpec((None, CONV_WIDTH - 1, CONV_DIM), lambda b, c: (b, 0, 0)),
                 pl.BlockSpec((None, SSM_HEADS, SSM_HEAD_DIM, SSM_STATE), lambda b, c: (b, 0, 0, 0)))
    scratch = [pltpu.VMEM((Q + 8, CONV_DIM), F32),
               pltpu.VMEM((SSM_GROUPS, SSM_STATE, D_INNER // SSM_GROUPS), F32),
               pltpu.VMEM((Q, D_INNER), F32), pltpu.VMEM((Q, CONV_DIM), F32), pltpu.VMEM((Q, D_INNER), BF16)]
    return pl.pallas_call(
        functools.partial(_ssd_body, Q=Q, nc=nc, has_init=has_init), out_shape=out_shape,
        grid=(n_seq, nc), in_specs=in_specs, out_specs=out_specs, scratch_shapes=scratch,
        input_output_aliases=aliases, compiler_params=_params(2, 40),
        name="ssd_sample" if has_init else "ssd_prompt")(*args)


def _attend(q_ref, sink_ref, o_ref, kband, vband, bias):
    ones = jnp.ones((kband.shape[1], HEAD_DIM), BF16)
    for h in range(N_Q_HEADS):
        kh = h // Q_PER_KV
        hd = slice(h * HEAD_DIM, (h + 1) * HEAD_DIM)
        s = _dot_nt(q_ref[:, hd], kband[kh])
        if bias is not None:
            s = s + bias
        sink = sink_ref[h]
        m = jnp.maximum(jnp.max(s, axis=-1, keepdims=True), sink)
        e = jnp.exp(s - m).astype(BF16)
        den = _dot(e, ones) + jnp.exp(sink - m)
        o_ref[:, hd] = (_dot(e, vband[kh]) / den).astype(o_ref.dtype)


def _stage_kv(kband, vband, row0, k_of, v_of):
    n = k_of(0).shape[0]
    for kh in range(N_KV_HEADS):
        kband[kh, row0:row0 + n, :] = k_of(kh).astype(BF16)
        vband[kh, row0:row0 + n, :] = v_of(kh).astype(BF16)


def _attn_prompt_body(sink_ref, q_ref, kv0_ref, kv1_ref, kv2_ref, o_ref, kband, vband):
    c = pl.program_id(1)
    KW = N_KV_HEADS * HEAD_DIM
    for j, ref in enumerate((kv0_ref, kv1_ref, kv2_ref)):
        _stage_kv(kband, vband, j * CHUNK,
                  functools.partial(lambda kh, r: r[:, kh * HEAD_DIM:(kh + 1) * HEAD_DIM], r=ref),
                  functools.partial(lambda kh, r: r[:, KW + kh * HEAD_DIM:KW + (kh + 1) * HEAD_DIM], r=ref))
    key = lax.broadcasted_iota(I32, (1, kband.shape[1]), 1)
    bias = jnp.where(key >= jnp.maximum(2 - c, 0) * CHUNK, 0.0, -jnp.inf).astype(F32)
    _attend(q_ref, sink_ref, o_ref, kband, vband, bias)


def _attn_sample_body(sink_ref, q_ref, ck_ref, cv_ref, kvn_ref, _, o_ref, kband, vband):
    KW = N_KV_HEADS * HEAD_DIM
    _stage_kv(kband, vband, 0, lambda kh: ck_ref[:, kh * HEAD_DIM:(kh + 1) * HEAD_DIM],
              lambda kh: cv_ref[:, kh * HEAD_DIM:(kh + 1) * HEAD_DIM])
    _stage_kv(kband, vband, WINDOW, lambda kh: kvn_ref[:, kh * HEAD_DIM:(kh + 1) * HEAD_DIM],
              lambda kh: kvn_ref[:, KW + kh * HEAD_DIM:KW + (kh + 1) * HEAD_DIM])
    _attend(q_ref, sink_ref, o_ref, kband, vband, None)


def _attention(q, kv, sinks, cache_k, cache_v, *, n_prompt, seq, n_dec, dec_len):
    T = q.shape[0]
    NQ = N_Q_HEADS * HEAD_DIM
    KW = N_KV_HEADS * HEAD_DIM
    nc = seq // CHUNK
    smem = pl.BlockSpec(memory_space=pltpu.SMEM)
    band = lambda back: (lambda b, c: (b * nc + jnp.maximum(c - back, 0), 0))
    bands = lambda keys: [pltpu.VMEM((N_KV_HEADS, keys, HEAD_DIM), BF16)] * 2
    o = pl.pallas_call(
        _attn_prompt_body, out_shape=jax.ShapeDtypeStruct((T, NQ), BF16), grid=(n_prompt, nc),
        in_specs=[smem, pl.BlockSpec((CHUNK, NQ), lambda b, c: (b * nc + c, 0)),
                  pl.BlockSpec((CHUNK, 2 * KW), band(2)), pl.BlockSpec((CHUNK, 2 * KW), band(1)),
                  pl.BlockSpec((CHUNK, 2 * KW), band(0))],
        out_specs=pl.BlockSpec((CHUNK, NQ), lambda b, c: (b * nc + c, 0)),
        scratch_shapes=bands(WINDOW + CHUNK),
        compiler_params=_params(2, 32), name="attn_prompt")(sinks, q, kv, kv, kv)
    blk0 = (n_prompt * seq) // dec_len
    return pl.pallas_call(
        _attn_sample_body, out_shape=jax.ShapeDtypeStruct((T, NQ), BF16), grid=(n_dec,),
        in_specs=[smem, pl.BlockSpec((dec_len, NQ), lambda b: (blk0 + b, 0)),
                  pl.BlockSpec((None, WINDOW, KW), lambda b: (b, 0, 0)),
                  pl.BlockSpec((None, WINDOW, KW), lambda b: (b, 0, 0)),
                  pl.BlockSpec((dec_len, 2 * KW), lambda b: (blk0 + b, 0)),
                  pl.BlockSpec(memory_space=pl.ANY)],
        out_specs=pl.BlockSpec((dec_len, NQ), lambda b: (blk0 + b, 0)),
        scratch_shapes=bands(WINDOW + dec_len),
        input_output_aliases={5: 0}, compiler_params=_params(1, 32), name="attn_sample")(
            sinks, q, cache_k.reshape(n_dec, WINDOW, KW), cache_v.reshape(n_dec, WINDOW, KW), kv, o)


def _router_body(x_ref, g_ref, rt_ref, idx_ref, w_ref):
    h = _rms_rows(x_ref[...], g_ref[...])
    lt = _dot_nt(rt_ref[...], h, HI)
    ids = lax.broadcasted_iota(I32, lt.shape, 0)
    m1 = jnp.max(lt, axis=0, keepdims=True)
    i1 = jnp.min(jnp.where(lt == m1, ids, N_EXPERTS), axis=0, keepdims=True)
    rest = jnp.where(ids == i1, -jnp.inf, lt)
    m2 = jnp.max(rest, axis=0, keepdims=True)
    i2 = jnp.min(jnp.where(rest == m2, ids, N_EXPERTS), axis=0, keepdims=True)
    e2 = jnp.exp(m2 - m1)
    w1 = 1.0 / (1.0 + e2)
    idx_ref[...] = jnp.concatenate([i1, i2], axis=0)
    w_ref[...] = jnp.concatenate([w1, e2 * w1], axis=0)


def _router(x, g, router):
    T = x.shape[0]
    tm = _pick(T, 512)
    return pl.pallas_call(
        _router_body, out_shape=(jax.ShapeDtypeStruct((2, T), I32), jax.ShapeDtypeStruct((2, T), F32)),
        grid=(T // tm,),
        in_specs=[pl.BlockSpec((tm, D_MODEL), lambda i: (i, 0)), pl.BlockSpec((1, D_MODEL), lambda i: (0, 0)),
                  pl.BlockSpec((N_EXPERTS, D_MODEL), lambda i: (0, 0))],
        out_specs=(pl.BlockSpec((2, tm), lambda i: (0, i)), pl.BlockSpec((2, tm), lambda i: (0, i))),
        compiler_params=_params(1, 32), name="moe_router")(x, g.reshape(1, D_MODEL), router.T)


def _route_plan(idx, tm, n_tiles):
    T = idx.shape[1]
    e_flat = idx.reshape(-1)
    onehot = (e_flat[:, None] == jnp.arange(N_EXPERTS, dtype=I32)[None, :]).astype(I32)
    csum = jnp.cumsum(onehot, axis=0)
    counts = csum[-1]
    rank = jnp.sum(csum * onehot, axis=1) - 1
    padded = ((counts + tm - 1) // tm) * tm
    ends = jnp.cumsum(padded)
    pos = jnp.sum((ends - padded)[None, :] * onehot, axis=1) + rank
    tok = jnp.tile(jnp.arange(T, dtype=I32), 2)
    src = jnp.zeros((n_tiles * tm,), I32).at[pos].set(tok)
    tile_start = jnp.arange(n_tiles, dtype=I32) * tm
    valid = tile_start < ends[-1]
    probe = jnp.minimum(tile_start, ends[-1] - 1)
    expert = jnp.minimum(jnp.sum((ends[None, :] <= probe[:, None]).astype(I32), axis=1), N_EXPERTS - 1)
    first = jnp.concatenate([jnp.ones((1,), I32), (expert[1:] != expert[:-1]).astype(I32)])
    real_end = jnp.sum((ends - padded + counts)[None, :] * (expert[:, None] == jnp.arange(N_EXPERTS, dtype=I32)[None, :]), axis=1)
    step = tm // MOE_ROW_SPLIT
    rows = jnp.clip(real_end - tile_start, 0, tm)
    rows = jnp.where(valid, ((rows + step - 1) // step) * step, 0)
    return pos.reshape(2, T).astype(I32), src, expert, rows.astype(I32), first


def _row_copy(table_hbm, row, buf, r, sem):
    return pltpu.make_async_copy(table_hbm.at[pl.ds(row, 1)], buf.at[pl.ds(r, 1)], sem)


def _start_rows(index_of, n_rows, table_hbm, buf, sem):
    def start(r2, carry):
        for k in range(2):
            r = 2 * r2 + k
            _row_copy(table_hbm, index_of(r), buf, r, sem).start(priority=k)
        return carry
    lax.fori_loop(0, n_rows // 2, start, 0, unroll=DMA_UNROLL // 2)


def _wait_rows(n_rows, table_hbm, buf, sem):
    def wait(r, carry):
        _row_copy(table_hbm, 0, buf, r, sem).wait()
        return carry
    lax.fori_loop(0, n_rows, wait, 0, unroll=DMA_UNROLL)


def _moe_gather_body(src_ref, nxt_ref, x_hbm, g_ref, o_ref, buf, sem):
    i = pl.program_id(0)
    tm = buf.shape[1]
    slot = i % 2

    @pl.when(i == 0)
    def _():
        _start_rows(lambda r: src_ref[0, 0, r], tm, x_hbm, buf.at[0], sem.at[0])

    @pl.when(i + 1 < pl.num_programs(0))
    def _():
        _start_rows(lambda r: nxt_ref[0, 0, r], tm, x_hbm, buf.at[1 - slot], sem.at[1 - slot])

    _wait_rows(tm, x_hbm, buf.at[slot], sem.at[slot])
    o_ref[...] = _rms_rows(buf[slot], g_ref[...]).astype(o_ref.dtype)


def _moe_gather(x, g, src, tm):
    n_tiles = src.shape[0] // tm
    src3 = src.reshape(n_tiles, 1, tm)
    return pl.pallas_call(
        _moe_gather_body, out_shape=jax.ShapeDtypeStruct((n_tiles * tm, D_MODEL), BF16), grid=(n_tiles,),
        in_specs=[pl.BlockSpec((1, 1, tm), lambda i: (i, 0, 0), memory_space=pltpu.SMEM),
                  pl.BlockSpec((1, 1, tm), lambda i: (jnp.minimum(i + 1, n_tiles - 1), 0, 0), memory_space=pltpu.SMEM),
                  pl.BlockSpec(memory_space=pl.ANY), pl.BlockSpec((1, D_MODEL), lambda i: (0, 0))],
        out_specs=pl.BlockSpec((tm, D_MODEL), lambda i: (i, 0)),
        scratch_shapes=[pltpu.VMEM((2, tm, D_MODEL), F32), pltpu.SemaphoreType.DMA((2,))],
        compiler_params=_params(1, 32), name="moe_gather")(src3, src3, x, g.reshape(1, D_MODEL))


def _for_leading_rows(rows, o_ref, compute):
    tm = o_ref.shape[0]
    step = tm // MOE_ROW_SPLIT
    for n in range(0, tm + 1, step):
        @pl.when(rows == n)
        def _(n=n):
            if n:
                o_ref[0:n, :] = compute(n).astype(o_ref.dtype)
            if n < tm:
                o_ref[n:tm, :] = jnp.zeros((tm - n, o_ref.shape[1]), o_ref.dtype)


def _moe_up_body(te_ref, tr_ref, tf_ref, x_ref, wg_ref, wu_ref, o_ref, wg_bf, wu_bf):
    i = pl.program_id(1)

    @pl.when(tf_ref[i] != 0)
    def _():
        wg_bf[...] = wg_ref[...].astype(BF16)
        wu_bf[...] = wu_ref[...].astype(BF16)

    def swiglu(n):
        x = x_ref[0:n, :]
        g = _dot(x, wg_bf[...])
        return g * _sigmoid(g) * _dot(x, wu_bf[...])

    _for_leading_rows(tr_ref[i], o_ref, swiglu)


def _moe_down_body(te_ref, tr_ref, x_ref, w_ref, o_ref):
    i = pl.program_id(1)
    _for_leading_rows(tr_ref[i], o_ref, lambda n: _dot(x_ref[0:n, :], w_ref[...]))


def _moe_experts(xs, w_gate_up, w_down, tile_expert, tile_rows, tile_first, tm):
    P = xs.shape[0]
    n_tiles = P // tm
    tn = _pick(EXPERT_FF, 1024)
    up_blocks = EXPERT_FF // tn
    act = pl.pallas_call(
        _moe_up_body, out_shape=jax.ShapeDtypeStruct((P, EXPERT_FF), BF16),
        grid_spec=pltpu.PrefetchScalarGridSpec(
            num_scalar_prefetch=3, grid=(up_blocks, n_tiles),
            in_specs=[pl.BlockSpec((tm, D_MODEL), lambda j, i, te, tv, tf: (i, 0)),
                      pl.BlockSpec((None, D_MODEL, tn), lambda j, i, te, tv, tf: (te[i], 0, j)),
                      pl.BlockSpec((None, D_MODEL, tn), lambda j, i, te, tv, tf: (te[i], 0, j + up_blocks))],
            out_specs=pl.BlockSpec((tm, tn), lambda j, i, te, tv, tf: (i, j)),
            scratch_shapes=[pltpu.VMEM((D_MODEL, tn), BF16)] * 2),
        compiler_params=_params(2, 60), name="moe_up")(tile_expert, tile_rows, tile_first, xs, w_gate_up, w_gate_up)
    tn = _pick(D_MODEL, 1024)
    return pl.pallas_call(
        _moe_down_body, out_shape=jax.ShapeDtypeStruct((P, D_MODEL), F32),
        grid_spec=pltpu.PrefetchScalarGridSpec(
            num_scalar_prefetch=2, grid=(D_MODEL // tn, n_tiles),
            in_specs=[pl.BlockSpec((tm, EXPERT_FF), lambda j, i, te, tv: (i, 0)),
                      pl.BlockSpec((None, EXPERT_FF, tn), lambda j, i, te, tv: (te[i], 0, j))],
            out_specs=pl.BlockSpec((tm, tn), lambda j, i, te, tv: (i, j))),
        compiler_params=_params(2, 56), name="moe_down")(tile_expert, tile_rows, act, w_down)


def _moe_combine_body(pos_ref, nxt_ref, x_ref, w_ref, ys_hbm, g_ref, xo_ref, ho_ref, buf, sem):
    i = pl.program_id(0)
    tc = buf.shape[2]
    slot = i % 2

    def start(table, s):
        for k in range(2):
            _start_rows(functools.partial(lambda r, k: table[0, k, r], k=k), tc, ys_hbm, buf.at[s, k], sem.at[s])

    @pl.when(i == 0)
    def _():
        start(pos_ref, 0)

    @pl.when(i + 1 < pl.num_programs(0))
    def _():
        start(nxt_ref, 1 - slot)

    for k in range(2):
        _wait_rows(tc, ys_hbm, buf.at[slot, k], sem.at[slot])
    w = w_ref[...]
    x = x_ref[...] + (w[:, 0:1] * buf[slot, 0] + w[:, 1:2] * buf[slot, 1])
    xo_ref[...] = x
    ho_ref[...] = _rms_rows(x, g_ref[...]).astype(ho_ref.dtype)


def _moe_combine(x, ys, pos, w, g):
    T = x.shape[0]
    tc = _pick(T, 256)
    n = T // tc
    pos_t = pos.reshape(2, n, tc).transpose(1, 0, 2)
    return pl.pallas_call(
        _moe_combine_body,
        out_shape=(jax.ShapeDtypeStruct((T, D_MODEL), F32), jax.ShapeDtypeStruct((T, D_MODEL), BF16)),
        grid=(n,),
        in_specs=[pl.BlockSpec((1, 2, tc), lambda i: (i, 0, 0), memory_space=pltpu.SMEM),
                  pl.BlockSpec((1, 2, tc), lambda i: (jnp.minimum(i + 1, n - 1), 0, 0), memory_space=pltpu.SMEM),
                  pl.BlockSpec((tc, D_MODEL), lambda i: (i, 0)), pl.BlockSpec((tc, 2), lambda i: (i, 0)),
                  pl.BlockSpec(memory_space=pl.ANY), pl.BlockSpec((1, D_MODEL), lambda i: (0, 0))],
        out_specs=(pl.BlockSpec((tc, D_MODEL), lambda i: (i, 0)), pl.BlockSpec((tc, D_MODEL), lambda i: (i, 0))),
        scratch_shapes=[pltpu.VMEM((2, 2, tc, D_MODEL), F32), pltpu.SemaphoreType.DMA((2,))],
        compiler_params=_params(1, 40), name="moe_combine")(pos_t, pos_t, x, w.T, ys, g.reshape(1, D_MODEL))


def _rope_tables(pos):
    half = HEAD_DIM // 2
    inv_freq = ROPE_THETA ** (-jnp.arange(half, dtype=F32) * 2.0 / HEAD_DIM)
    ang = pos.astype(F32)[:, None] * inv_freq[None, :]
    cos, sin = jnp.cos(ang), jnp.sin(ang)
    reps = LANES // HEAD_DIM
    return jnp.tile(jnp.concatenate([cos, cos], axis=-1), (1, reps)), jnp.tile(jnp.concatenate([-sin, sin], axis=-1), (1, reps))


def kernel(x_prompt, x_sample, p_prompt, p_sample, cache_conv, state_ssm, cache_k, cache_v, norm_mix, norm_ffn, norm_ple, norm_final, mamba_in_proj, mamba_conv_w, mamba_conv_b, mamba_dt_bias, mamba_a_log, mamba_d, mamba_norm, mamba_out_proj, norm_kv, w_kv, w_q, attn_sinks, w_o, ffn_w_gate_up, ffn_w_down, moe_router, moe_w_gate_up, moe_w_down, ple_proj, ple_gate):
    n_prompt, seq, _ = x_prompt.shape
    n_dec, dec_len, _ = x_sample.shape
    Tp = n_prompt * seq
    T = Tp + n_dec * dec_len
    assert seq % CHUNK == 0 and Tp % dec_len == 0 and dec_len % 8 == 0

    bf = lambda w: w.astype(BF16)
    x0 = jnp.concatenate([x_prompt.reshape(Tp, D_MODEL), x_sample.reshape(-1, D_MODEL)], axis=0)
    p_all = jnp.concatenate([p_prompt.reshape(2, Tp, PLE_DIM), p_sample.reshape(2, -1, PLE_DIM)], axis=1).astype(BF16)
    pos = jnp.concatenate([jnp.tile(jnp.arange(seq, dtype=I32), n_prompt),
                           PAST_LEN + jnp.tile(jnp.arange(dec_len, dtype=I32), n_dec)])
    cos_t, sin_t = _rope_tables(pos)

    tm_wide = _pick(T, 1536)
    tm_mid = _pick(T, 768)
    tm_deep = _pick(T, 512)

    w_in = mamba_in_proj[0]
    h = _rmsnorm(x0, norm_mix[0], BF16)
    tn = _pick(D_INNER, 1024)
    z = _mm_call(_mm_plain_body, T, D_INNER, tm_wide, tn, [("lhs", h), ("w", w_in, 0)], BF16, 56, "in_proj_z")
    xbc = _mm_call(_mm_plain_body, T, CONV_DIM, tm_wide, tn, [("lhs", h), ("w", w_in, D_INNER // tn)], F32, 56, "in_proj_xbc")
    dt = _mm_call(_mm_plain_body, T, SSM_HEADS, tm_wide, SSM_HEADS,
                  [("lhs", h), ("w", w_in[:, D_INNER + CONV_DIM:], 0)], F32, 32, "in_proj_dt")
    ssd_prm = dict(conv_w=mamba_conv_w[0], conv_b=mamba_conv_b[0].reshape(1, CONV_DIM),
                   dt_bias=mamba_dt_bias[0].reshape(1, SSM_HEADS), a_log=mamba_a_log[0].reshape(1, SSM_HEADS),
                   d_full=jnp.repeat(mamba_d[0], SSM_HEAD_DIM).reshape(1, D_INNER),
                   norm_w=mamba_norm[0].reshape(1, D_INNER))
    g, conv_prompt, ssm_prompt = _ssd_call(z, xbc, dt, ssd_prm, n_seq=n_prompt, seq_len=seq, row0=0, Q=CHUNK)
    g, conv_sample, ssm_sample = _ssd_call(z, xbc, dt, ssd_prm, n_seq=n_dec, seq_len=dec_len, row0=Tp, Q=dec_len,
                                           conv0=cache_conv[0], ssm0=state_ssm[0], g_in=g)
    tn = _pick(D_MODEL, 512)
    x1 = _mm_call(_mm_res_body, T, D_MODEL, tm_mid, tn,
                  [("lhs", g), ("w", mamba_out_proj[0], 0), ("row", x0, True)], F32, 48, "out_proj")

    h = _rmsnorm(x1, norm_ffn[0], BF16)
    tn = _pick(D_FF, 512)
    w_gu = ffn_w_gate_up[0]
    a = _mm_call(_mm_swiglu_body, T, D_FF, tm_wide, tn,
                 [("lhs", h), ("w", w_gu, 0), ("w", w_gu, D_FF // tn)], BF16, 56, "ffn_up")
    tn = _pick(D_MODEL, 512)
    x2 = _mm_call(_mm_res_body, T, D_MODEL, tm_deep, tn,
                  [("lhs", a), ("w", ffn_w_down[0], 0), ("row", x1, True)], F32, 52, "ffn_down")
    h = _rmsnorm(x2, norm_ple[0], BF16)
    tn = _pick(D_MODEL, 512)
    x3 = _mm_call(_mm_ple_body, T, D_MODEL, tm_wide, tn,
                  [("lhs", h), ("w", ple_gate[0], 0), ("lhs", p_all[0]), ("w", ple_proj[0], 0), ("row", x2, True)],
                  F32, 48, "ple0")

    KW = N_KV_HEADS * HEAD_DIM
    hk, hq = _rmsnorm2(x3, norm_kv, norm_mix[1], BF16)
    kv = _mm_call(functools.partial(_mm_rope_body, n_rope=KW // LANES, scale=1.0), T, 2 * KW, tm_wide, 2 * KW,
                  [("lhs", hk), ("w", w_kv, 0), ("row", cos_t, False), ("row", sin_t, False)], F32, 48, "kv_proj")
    tn = _pick(D_MODEL, 512)
    q = _mm_call(functools.partial(_mm_rope_body, n_rope=tn // LANES, scale=HEAD_DIM ** -0.5), T, D_MODEL, tm_wide, tn,
                 [("lhs", hq), ("w", w_q[0], 0), ("row", cos_t, False), ("row", sin_t, False)], BF16, 48, "q_proj")
    o = _attention(q, kv, attn_sinks[0], cache_k, cache_v, n_prompt=n_prompt, seq=seq, n_dec=n_dec, dec_len=dec_len)
    tn = _pick(D_MODEL, 1024)
    x4 = _mm_call(_mm_res_body, T, D_MODEL, tm_mid, tn,
                  [("lhs", o), ("w", w_o[0], 0), ("row", x3, True)], F32, 48, "o_proj")

    tm_moe = _pick(T, 512)
    n_tiles = (2 * T) // tm_moe + N_EXPERTS
    idx, gate_w = _router(x4, norm_ffn[1], moe_router[0])
    pos_sorted, src, tile_expert, tile_rows, tile_first = _route_plan(idx, tm_moe, n_tiles)
    xs = _moe_gather(x4, norm_ffn[1], src, tm_moe)
    ys = _moe_experts(xs, moe_w_gate_up[0], bf(moe_w_down[0]), tile_expert, tile_rows, tile_first, tm_moe)
    x5, h = _moe_combine(x4, ys, pos_sorted, gate_w, norm_ple[1])
    tn = _pick(D_MODEL, 512)
    x6 = _mm_call(_mm_ple_body, T, D_MODEL, tm_wide, tn,
                  [("lhs", h), ("w", ple_gate[1], 0), ("lhs", p_all[1]), ("w", ple_proj[1], 0), ("row", x5, True)],
                  F32, 48, "ple1")
    y_prompt, y_sample = _rmsnorm_split(x6, norm_final, Tp)

    k_all = kv[:, :KW]
    v_all = kv[:, KW:]
    tail = lambda a: a[:Tp].reshape(n_prompt, seq, N_KV_HEADS, HEAD_DIM)[:, seq - WINDOW:]
    new = lambda a: a[Tp:].reshape(n_dec, dec_len, N_KV_HEADS, HEAD_DIM)
    k_sample = jnp.concatenate([cache_k, new(k_all)], axis=1)[:, -WINDOW:]
    v_sample = jnp.concatenate([cache_v, new(v_all)], axis=1)[:, -WINDOW:]
    return (y_prompt.reshape(n_prompt, seq, D_MODEL), y_sample.reshape(n_dec, dec_len, D_MODEL),
            conv_prompt[None], ssm_prompt[None], tail(k_all), tail(v_all),
            conv_sample[None], ssm_sample[None], k_sample, v_sample)
```

```python
import functools

import jax
import jax.numpy as jnp
from jax import lax
from jax.experimental import pallas as pl
from jax.experimental.pallas import tpu as pltpu

F32 = jnp.float32
BF16 = jnp.bfloat16
I32 = jnp.int32
HI = lax.Precision.HIGHEST

D_MODEL = 2048
D_INNER = 4096
SSM_HEADS = 64
SSM_HEAD_DIM = 64
SSM_GROUPS = 8
SSM_STATE = 128
CONV_WIDTH = 4
CONV_DIM = D_INNER + 2 * SSM_GROUPS * SSM_STATE
N_Q_HEADS = 32
N_KV_HEADS = 4
Q_PER_KV = N_Q_HEADS // N_KV_HEADS
HEAD_DIM = 64
WINDOW = 128
CHUNK = 64
ROPE_THETA = 10000.0
D_FF = 5632
N_EXPERTS = 8
EXPERT_FF = 7168
PLE_DIM = 256
PAST_LEN = 4096
EPS = 1e-6

LANES = 128
MIB = 1024 * 1024
NT_DIMS = (((1,), (1,)), ((), ()))
DMA_UNROLL = 8
CONV_COLS = 512
PAIR_W = 2 * SSM_HEAD_DIM
MOE_ROW_SPLIT = 4


def _params(n_axes, vmem_mib):
    return pltpu.CompilerParams(dimension_semantics=("arbitrary",) * n_axes,
                                vmem_limit_bytes=vmem_mib * MIB)


def _pick(n, pref):
    for t in (1536, 1024, 768, 512, 384, 256, 128, 96, 64, 32, 16, 8):
        if t <= pref and n % t == 0:
            return t
    raise ValueError(f"no tile for {n}")


def _dot(a, b):
    return jnp.dot(a, b, preferred_element_type=F32)


def _dot_nt(a, b, precision=None):
    return lax.dot_general(a, b, NT_DIMS, precision=precision, preferred_element_type=F32)


def _sigmoid(x):
    return 1.0 / (1.0 + jnp.exp(-x))


def _softplus(x):
    return jnp.maximum(x, 0.0) + jnp.log1p(jnp.exp(-jnp.abs(x)))


def _eye(n, dtype):
    r = lax.broadcasted_iota(I32, (n, n), 0)
    c = lax.broadcasted_iota(I32, (n, n), 1)
    return (r == c).astype(dtype)


def _rms_rows(x, g):
    inv = lax.rsqrt(jnp.mean(x * x, axis=-1, keepdims=True) + EPS)
    return x * inv * g


def _rms_body(x_ref, g_ref, o_ref):
    o_ref[...] = _rms_rows(x_ref[...], g_ref[...]).astype(o_ref.dtype)


def _rmsnorm(x, g, out_dtype):
    T, D = x.shape
    tr = _pick(T, 512)
    return pl.pallas_call(
        _rms_body, out_shape=jax.ShapeDtypeStruct((T, D), out_dtype), grid=(T // tr,),
        in_specs=[pl.BlockSpec((tr, D), lambda i: (i, 0)), pl.BlockSpec((1, D), lambda i: (0, 0))],
        out_specs=pl.BlockSpec((tr, D), lambda i: (i, 0)),
        compiler_params=_params(1, 32), name="rmsnorm")(x, g.reshape(1, D))


def _rms_split_body(x_ref, g_ref, o1_ref, o2_ref, *, n_first):
    i = pl.program_id(0)
    y = _rms_rows(x_ref[...], g_ref[...])

    @pl.when(i < n_first)
    def _():
        o1_ref[...] = y

    @pl.when(i >= n_first)
    def _():
        o2_ref[...] = y


def _rmsnorm_split(x, g, rows_first):
    T, D = x.shape
    rest = T - rows_first
    tr = _pick(rows_first, 512)
    while rest % tr:
        tr //= 2
    n_first = rows_first // tr
    return pl.pallas_call(
        functools.partial(_rms_split_body, n_first=n_first),
        out_shape=(jax.ShapeDtypeStruct((rows_first, D), F32), jax.ShapeDtypeStruct((rest, D), F32)), grid=(T // tr,),
        in_specs=[pl.BlockSpec((tr, D), lambda i: (i, 0)), pl.BlockSpec((1, D), lambda i: (0, 0))],
        out_specs=(pl.BlockSpec((tr, D), lambda i: (jnp.minimum(i, n_first - 1), 0)),
                   pl.BlockSpec((tr, D), lambda i: (jnp.maximum(i - n_first, 0), 0))),
        compiler_params=_params(1, 32), name="rmsnorm_final")(x, g.reshape(1, D))


def _mm_plain_body(x_ref, w_ref, o_ref):
    o_ref[...] = _dot(x_ref[...], w_ref[...]).astype(o_ref.dtype)


def _inv_rms(x_ref):
    x = x_ref[...].astype(F32)
    return lax.rsqrt(jnp.mean(x * x, axis=-1, keepdims=True) + EPS)


def _store_with_copy(y, o_ref, copy_ref):
    o_ref[...] = y
    if copy_ref is not None:
        copy_ref[...] = y.astype(copy_ref.dtype)


def _mm_res_body(x_ref, w_ref, r_ref, o_ref, copy_ref=None):
    _store_with_copy(r_ref[...] + _dot(x_ref[...], w_ref[...]), o_ref, copy_ref)


def _mm_swiglu_body(x_ref, wg_ref, wu_ref, o_ref, *, normed):
    x = x_ref[...]
    g = _dot(x, wg_ref[...])
    u = _dot(x, wu_ref[...])
    if normed:
        inv = _inv_rms(x_ref)
        g, u = g * inv, u * inv
    o_ref[...] = (g * _sigmoid(g) * u).astype(o_ref.dtype)


def _mm_ple_body(h_ref, wg_ref, p_ref, wp_ref, r_ref, o_ref, copy_ref=None, *, normed):
    a = _dot(h_ref[...], wg_ref[...])
    if normed:
        a = a * _inv_rms(h_ref)
    _store_with_copy(r_ref[...] + _sigmoid(a) * _dot(p_ref[...], wp_ref[...]), o_ref, copy_ref)


def _mm_rope_body(x_ref, w_ref, cos_ref, sin_ref, o_ref, *, n_rope, scale, normed):
    acc = _dot(x_ref[...], w_ref[...])
    if normed:
        acc = acc * _inv_rms(x_ref)
    tm, tn = acc.shape
    cos = cos_ref[...]
    sin = sin_ref[...]
    lane = lax.broadcasted_iota(I32, (tm, LANES), 1)
    first_half = (lane % HEAD_DIM) < (HEAD_DIM // 2)
    for c in range(tn // LANES):
        x = acc[:, c * LANES:(c + 1) * LANES]
        if c < n_rope:
            partner = jnp.where(first_half, pltpu.roll(x, LANES - HEAD_DIM // 2, 1),
                                pltpu.roll(x, HEAD_DIM // 2, 1))
            x = (x * cos + partner * sin) * scale
        o_ref[:, c * LANES:(c + 1) * LANES] = x.astype(o_ref.dtype)


def _with_weight_cast(body, w_slots, n_ops, n_in, n_out):
    def wrapped(*refs):
        ins, outs, scratch = list(refs[:n_ops]), refs[n_in:n_in + n_out], refs[n_in + n_out:]

        @pl.when(pl.program_id(1) == 0)
        def _():
            for (k, gk), s in zip(w_slots, scratch):
                w = ins[k][...]
                s[...] = (w if gk is None else w * refs[gk][...]).astype(BF16)

        for (k, _), s in zip(w_slots, scratch):
            ins[k] = s
        body(*ins, *outs)
    return wrapped


def _mm_call(body, T, N, tm, tn, operands, out_dtype, vmem_mib, name, copy_dtype=None):
    in_specs, args, w_slots, scratch, gains = [], [], [], [], []
    for op in operands:
        kind, a = op[0], op[1]
        if kind == "lhs":
            in_specs.append(pl.BlockSpec((tm, a.shape[1]), lambda j, i: (i, 0)))
        elif kind == "w":
            in_specs.append(pl.BlockSpec((a.shape[0], tn), functools.partial(lambda j, i, off: (0, j + off), off=op[2])))
            w_slots.append([len(args), None])
            if len(op) > 3:
                w_slots[-1][1] = len(operands) + len(gains)
                gains.append(op[3].reshape(-1, 1))
            scratch.append(pltpu.VMEM((a.shape[0], tn), BF16))
        elif op[2]:
            in_specs.append(pl.BlockSpec((tm, tn), lambda j, i: (i, j)))
        else:
            in_specs.append(pl.BlockSpec((tm, a.shape[1]), lambda j, i: (i, 0)))
        args.append(a)
    for g in gains:
        in_specs.append(pl.BlockSpec(g.shape, lambda j, i: (0, 0)))
        args.append(g)
    tile = pl.BlockSpec((tm, tn), lambda j, i: (i, j))
    out_shape = jax.ShapeDtypeStruct((T, N), out_dtype)
    n_out = 1
    if copy_dtype is not None:
        out_shape, tile, n_out = (out_shape, jax.ShapeDtypeStruct((T, N), copy_dtype)), (tile, tile), 2
    return pl.pallas_call(
        _with_weight_cast(body, w_slots, len(operands), len(args), n_out), out_shape=out_shape,
        grid=(N // tn, T // tm), in_specs=in_specs, out_specs=tile,
        scratch_shapes=scratch, compiler_params=_params(2, vmem_mib), name=name)(*args)


def _ssd_body(*refs, Q, nc, has_init):
    if has_init:
        (z_ref, xbc_ref, dt_ref, cw_ref, cb_ref, dtb_ref, alog_ref, dfull_ref, nw_ref, conv0_ref, ssm0_ref,
         _, g_ref, convo_ref, ssmo_ref, xp, HT, ysc, act, xsb) = refs
    else:
        (z_ref, xbc_ref, dt_ref, cw_ref, cb_ref, dtb_ref, alog_ref, dfull_ref, nw_ref,
         g_ref, convo_ref, ssmo_ref, xp, HT, ysc, act, xsb) = refs
    c = pl.program_id(1)
    HPG = SSM_HEADS // SSM_GROUPS
    eye_n_bf = _eye(SSM_STATE, BF16)

    @pl.when(c == 0)
    def _init():
        xp[0:8, :] = jnp.zeros((8, CONV_DIM), F32)
        if has_init:
            xp[8 - (CONV_WIDTH - 1):8, :] = conv0_ref[...]
            for g in range(SSM_GROUPS):
                HT[g] = ssm0_ref[g * HPG:(g + 1) * HPG].reshape(HPG * SSM_HEAD_DIM, SSM_STATE).T
        else:
            HT[...] = jnp.zeros(HT.shape, F32)

    xp[8:8 + Q, :] = xbc_ref[...]
    for k in range(CONV_DIM // CONV_COLS):
        cs = slice(k * CONV_COLS, (k + 1) * CONV_COLS)
        xe = xp[:, cs]
        cw = cw_ref[:, cs]
        conv = cb_ref[:, cs]
        for t in range(CONV_WIDTH - 1):
            conv = conv + pltpu.roll(xe, CONV_WIDTH - 1 - t, 0)[8:, :] * cw[t:t + 1, :]
        conv = conv + xe[8:, :] * cw[CONV_WIDTH - 1:CONV_WIDTH, :]
        a = conv * _sigmoid(conv)
        act[:, cs] = a
        if k * CONV_COLS < D_INNER:
            xsb[:, cs] = a.astype(BF16)

    @pl.when(c == nc - 1)
    def _conv_out():
        convo_ref[...] = xp[8 + Q - (CONV_WIDTH - 1):8 + Q, :]

    xp[0:8, :] = xp[Q:Q + 8, :]

    dt = _softplus(dt_ref[...] + dtb_ref[...])
    dA = dt * (-jnp.exp(alog_ref[...]))
    tril = lax.broadcasted_iota(I32, (Q, Q), 0) >= lax.broadcasted_iota(I32, (Q, Q), 1)
    a_cs = jnp.dot(tril.astype(F32), dA, precision=HI, preferred_element_type=F32)
    a_last = a_cs[Q - 1:Q, :]
    w_end = jnp.exp(a_last - a_cs) * dt
    cdec = jnp.exp(a_last)

    n_pairs = SSM_HEADS // 2
    pr = lax.broadcasted_iota(I32, (n_pairs, SSM_HEADS), 0)
    pc_ = lax.broadcasted_iota(I32, (n_pairs, SSM_HEADS), 1)
    sel_a = (pc_ == 2 * pr).astype(F32)
    sel_b = (pc_ == 2 * pr + 1).astype(F32)

    def pair_rows(m):
        return jnp.concatenate([_dot_nt(sel_a, m, HI), _dot_nt(sel_b, m, HI)], axis=1)

    a_csT2, dtT2, w_endT2 = pair_rows(a_cs), pair_rows(dt), pair_rows(w_end)
    tok_r =lax.broadcasted_iota(I32, (Q, 2 * Q), 0)
    tok_c = lax.broadcasted_iota(I32, (Q, 2 * Q), 1)
    first_tok = tok_c < Q
    tril2 = tok_r >= jnp.where(first_tok, tok_c, tok_c - Q)
    first_ch = lax.broadcasted_iota(I32, (Q, PAIR_W), 1) < SSM_HEAD_DIM
    first_ch2 = lax.broadcasted_iota(I32, (2 * Q, PAIR_W), 1) < SSM_HEAD_DIM
    first_row = lax.broadcasted_iota(I32, (2 * Q, PAIR_W), 0) < Q
    ssq = jnp.zeros((Q, PAIR_W), F32)

    for g in range(SSM_GROUPS):
        b0 = D_INNER + g * SSM_STATE
        c0 = D_INNER + SSM_GROUPS * SSM_STATE + g * SSM_STATE
        Bg = act[:, b0:b0 + SSM_STATE].astype(BF16)
        Cg = act[:, c0:c0 + SSM_STATE].astype(BF16)
        cb = _dot_nt(Cg, Bg)
        cb2 = jnp.concatenate([cb, cb], axis=1)
        BT = _dot_nt(eye_n_bf, Bg)
        BT2 = jnp.concatenate([BT, BT], axis=1)
        Hg = HT[g]
        yoff = _dot(Cg, Hg.astype(BF16))
        for pp in range(HPG // 2):
            i = g * (HPG // 2) + pp
            ha, hb = 2 * i, 2 * i + 1
            pc = slice(i * PAIR_W, (i + 1) * PAIR_W)
            gc = slice(pp * PAIR_W, (pp + 1) * PAIR_W)
            col_a, col_b = a_cs[:, ha:ha + 1], a_cs[:, hb:hb + 1]
            seg = jnp.where(first_tok, col_a, col_b) - a_csT2[i:i + 1, :]
            dec = jnp.where(tril2, jnp.exp(seg), 0.0)
            M2 = (cb2 * dec * dtT2[i:i + 1, :]).astype(BF16)
            BTw2 = (BT2 * w_endT2[i:i + 1, :]).astype(BF16)
            x2 = xsb[:, pc]
            x2 = jnp.concatenate([x2, x2], axis=0)
            xbd = jnp.where(first_row == first_ch2, x2, jnp.zeros_like(x2))
            r = _dot(jnp.concatenate([M2, BTw2], axis=0), xbd)
            ea2 = jnp.exp(jnp.where(first_ch, col_a, col_b))
            y = r[:Q] + yoff[:, gc] * ea2 + act[:, pc] * dfull_ref[:, pc]
            zf = z_ref[:, pc].astype(F32)
            y = y * (zf * _sigmoid(zf))
            ysc[:, pc] = y
            ssq = ssq + y * y
            cd2 = jnp.where(first_ch[0:1, :], cdec[:, ha:ha + 1], cdec[:, hb:hb + 1])
            HT[g, :, gc] = Hg[:, gc] * cd2 + r[Q:]

    inv = lax.rsqrt(jnp.sum(ssq, axis=-1, keepdims=True) * (1.0 / D_INNER) + EPS)
    g_ref[...] = (ysc[...] * inv * nw_ref[...]).astype(g_ref.dtype)

    @pl.when(c == nc - 1)
    def _state_out():
        for g in range(SSM_GROUPS):
            ssmo_ref[g * HPG:(g + 1) * HPG] = HT[g].T.reshape(HPG, SSM_HEAD_DIM, SSM_STATE)


def _ssd_call(z, xbc, dt, prm, *, n_seq, seq_len, row0, Q, conv0=None, ssm0=None, g_in=None):
    T = z.shape[0]
    nc = seq_len // Q
    blk0 = row0 // Q
    has_init = conv0 is not None
    rows = lambda b, c: (blk0 + b * nc + c, 0)
    const = lambda b, c: (0, 0)
    in_specs = [pl.BlockSpec((Q, D_INNER), rows), pl.BlockSpec((Q, CONV_DIM), rows), pl.BlockSpec((Q, SSM_HEADS), rows),
                pl.BlockSpec((CONV_WIDTH, CONV_DIM), const), pl.BlockSpec((1, CONV_DIM), const),
                pl.BlockSpec((1, SSM_HEADS), const), pl.BlockSpec((1, SSM_HEADS), const),
                pl.BlockSpec((1, D_INNER), const), pl.BlockSpec((1, D_INNER), const)]
    args = [z, xbc, dt, prm["conv_w"], prm["conv_b"], prm["dt_bias"], prm["a_log"], prm["d_full"], prm["norm_w"]]
    aliases = {}
    if has_init:
        in_specs += [pl.BlockSpec((None, CONV_WIDTH - 1, CONV_DIM), lambda b, c: (b, 0, 0)),
                     pl.BlockSpec((None, SSM_HEADS, SSM_HEAD_DIM, SSM_STATE), lambda b, c: (b, 0, 0, 0)),
                     pl.BlockSpec(memory_space=pl.ANY)]
        args += [conv0, ssm0, g_in]
        aliases = {len(args) - 1: 0}
    out_shape = (jax.ShapeDtypeStruct((T, D_INNER), BF16),
                 jax.ShapeDtypeStruct((n_seq, CONV_WIDTH - 1, CONV_DIM), F32),
                 jax.ShapeDtypeStruct((n_seq, SSM_HEADS, SSM_HEAD_DIM, SSM_STATE), F32))
    out_specs = (pl.BlockSpec((Q, D_INNER), rows),
                 pl.BlockSpec((None, CONV_WIDTH - 1, CONV_DIM), lambda b, c: (b, 0, 0)),
                 pl.BlockSpec((None, SSM_HEADS, SSM_HEAD_DIM, SSM_STATE), lambda b, c: (b, 0, 0, 0)))
    scratch = [pltpu.VMEM((Q + 8, CONV_DIM), F32),
               pltpu.VMEM((SSM_GROUPS, SSM_STATE, D_INNER // SSM_GROUPS), F32),
               pltpu.VMEM((Q, D_INNER), F32), pltpu.VMEM((Q, CONV_DIM), F32), pltpu.VMEM((Q, D_INNER), BF16)]
    return pl.pallas_call(
        functools.partial(_ssd_body, Q=Q, nc=nc, has_init=has_init), out_shape=out_shape,
        grid=(n_seq, nc), in_specs=in_specs, out_specs=out_specs, scratch_shapes=scratch,
        input_output_aliases=aliases, compiler_params=_params(2, 40),
        name="ssd_sample" if has_init else "ssd_prompt")(*args)


def _attend(q_ref, sink_ref, o_ref, kband, vband, bias):
    ones = jnp.ones((kband.shape[1], HEAD_DIM), BF16)
    for h in range(N_Q_HEADS):
        kh = h // Q_PER_KV
        hd = slice(h * HEAD_DIM, (h + 1) * HEAD_DIM)
        s = _dot_nt(q_ref[:, hd], kband[kh])
        if bias is not None:
            s = s + bias
        sink = sink_ref[h]
        m = jnp.maximum(jnp.max(s, axis=-1, keepdims=True), sink)
        e = jnp.exp(s - m).astype(BF16)
        den = _dot(e, ones) + jnp.exp(sink - m)
        o_ref[:, hd] = (_dot(e, vband[kh]) / den).astype(o_ref.dtype)


def _stage_kv(kband, vband, row0, k_of, v_of):
    n = k_of(0).shape[0]
    for kh in range(N_KV_HEADS):
        kband[kh, row0:row0 + n, :] = k_of(kh).astype(BF16)
        vband[kh, row0:row0 + n, :] = v_of(kh).astype(BF16)


def _attn_prompt_body(sink_ref, q_ref, kv0_ref, kv1_ref, kv2_ref, o_ref, kband, vband):
    c = pl.program_id(1)
    KW = N_KV_HEADS * HEAD_DIM
    for j, ref in enumerate((kv0_ref, kv1_ref, kv2_ref)):
        _stage_kv(kband, vband, j * CHUNK,
                  functools.partial(lambda kh, r: r[:, kh * HEAD_DIM:(kh + 1) * HEAD_DIM], r=ref),
                  functools.partial(lambda kh, r: r[:, KW + kh * HEAD_DIM:KW + (kh + 1) * HEAD_DIM], r=ref))
    key = lax.broadcasted_iota(I32, (1, kband.shape[1]), 1)
    bias = jnp.where(key >= jnp.maximum(2 - c, 0) * CHUNK, 0.0, -jnp.inf).astype(F32)
    _attend(q_ref, sink_ref, o_ref, kband, vband, bias)


def _attn_sample_body(sink_ref, q_ref, ck_ref, cv_ref, kvn_ref, _, o_ref, kband, vband):
    KW = N_KV_HEADS * HEAD_DIM
    _stage_kv(kband, vband, 0, lambda kh: ck_ref[:, kh * HEAD_DIM:(kh + 1) * HEAD_DIM],
              lambda kh: cv_ref[:, kh * HEAD_DIM:(kh + 1) * HEAD_DIM])
    _stage_kv(kband, vband, WINDOW, lambda kh: kvn_ref[:, kh * HEAD_DIM:(kh + 1) * HEAD_DIM],
              lambda kh: kvn_ref[:, KW + kh * HEAD_DIM:KW + (kh + 1) * HEAD_DIM])
    _attend(q_ref, sink_ref, o_ref, kband, vband, None)


def _attention(q, kv, sinks, cache_k, cache_v, *, n_prompt, seq, n_dec, dec_len):
    T = q.shape[0]
    NQ = N_Q_HEADS * HEAD_DIM
    KW = N_KV_HEADS * HEAD_DIM
    nc = seq // CHUNK
    smem = pl.BlockSpec(memory_space=pltpu.SMEM)
    band = lambda back: (lambda b, c: (b * nc + jnp.maximum(c - back, 0), 0))
    bands = lambda keys: [pltpu.VMEM((N_KV_HEADS, keys, HEAD_DIM), BF16)] * 2
    o = pl.pallas_call(
        _attn_prompt_body, out_shape=jax.ShapeDtypeStruct((T, NQ), BF16), grid=(n_prompt, nc),
        in_specs=[smem, pl.BlockSpec((CHUNK, NQ), lambda b, c: (b * nc + c, 0)),
                  pl.BlockSpec((CHUNK, 2 * KW), band(2)), pl.BlockSpec((CHUNK, 2 * KW), band(1)),
                  pl.BlockSpec((CHUNK, 2 * KW), band(0))],
        out_specs=pl.BlockSpec((CHUNK, NQ), lambda b, c: (b * nc + c, 0)),
        scratch_shapes=bands(WINDOW + CHUNK),
        compiler_params=_params(2, 32), name="attn_prompt")(sinks, q, kv, kv, kv)
    blk0 = (n_prompt * seq) // dec_len
    return pl.pallas_call(
        _attn_sample_body, out_shape=jax.ShapeDtypeStruct((T, NQ), BF16), grid=(n_dec,),
        in_specs=[smem, pl.BlockSpec((dec_len, NQ), lambda b: (blk0 + b, 0)),
                  pl.BlockSpec((None, WINDOW, KW), lambda b: (b, 0, 0)),
                  pl.BlockSpec((None, WINDOW, KW), lambda b: (b, 0, 0)),
                  pl.BlockSpec((dec_len, 2 * KW), lambda b: (blk0 + b, 0)),
                  pl.BlockSpec(memory_space=pl.ANY)],
        out_specs=pl.BlockSpec((dec_len, NQ), lambda b: (blk0 + b, 0)),
        scratch_shapes=bands(WINDOW + dec_len),
        input_output_aliases={5: 0}, compiler_params=_params(1, 32), name="attn_sample")(
            sinks, q, cache_k.reshape(n_dec, WINDOW, KW), cache_v.reshape(n_dec, WINDOW, KW), kv, o)


def _router_body(x_ref, g_ref, rt_ref, idx_ref, w_ref):
    h = _rms_rows(x_ref[...], g_ref[...])
    lt = _dot_nt(rt_ref[...], h, HI)
    ids = lax.broadcasted_iota(I32, lt.shape, 0)
    m1 = jnp.max(lt, axis=0, keepdims=True)
    i1 = jnp.min(jnp.where(lt == m1, ids, N_EXPERTS), axis=0, keepdims=True)
    rest = jnp.where(ids == i1, -jnp.inf, lt)
    m2 = jnp.max(rest, axis=0, keepdims=True)
    i2 = jnp.min(jnp.where(rest == m2, ids, N_EXPERTS), axis=0, keepdims=True)
    e2 = jnp.exp(m2 - m1)
    w1 = 1.0 / (1.0 + e2)
    idx_ref[...] = jnp.concatenate([i1, i2], axis=0)
    w_ref[...] = jnp.concatenate([w1, e2 * w1], axis=0)


def _router(x, g, router):
    T = x.shape[0]
    tm = _pick(T, 512)
    return pl.pallas_call(
        _router_body, out_shape=(jax.ShapeDtypeStruct((2, T), I32), jax.ShapeDtypeStruct((2, T), F32)),
        grid=(T // tm,),
        in_specs=[pl.BlockSpec((tm, D_MODEL), lambda i: (i, 0)), pl.BlockSpec((1, D_MODEL), lambda i: (0, 0)),
                  pl.BlockSpec((N_EXPERTS, D_MODEL), lambda i: (0, 0))],
        out_specs=(pl.BlockSpec((2, tm), lambda i: (0, i)), pl.BlockSpec((2, tm), lambda i: (0, i))),
        compiler_params=_params(1, 32), name="moe_router")(x, g.reshape(1, D_MODEL), router.T)


def _route_plan(idx, tm, n_tiles):
    T = idx.shape[1]
    e_flat = idx.reshape(-1)
    onehot = (e_flat[:, None] == jnp.arange(N_EXPERTS, dtype=I32)[None, :]).astype(I32)
    csum = jnp.cumsum(onehot, axis=0)
    counts = csum[-1]
    rank = jnp.sum(csum * onehot, axis=1) - 1
    padded = ((counts + tm - 1) // tm) * tm
    ends = jnp.cumsum(padded)
    pos = jnp.sum((ends - padded)[None, :] * onehot, axis=1) + rank
    tok = jnp.tile(jnp.arange(T, dtype=I32), 2)
    src = jnp.zeros((n_tiles * tm,), I32).at[pos].set(tok)
    tile_start = jnp.arange(n_tiles, dtype=I32) * tm
    valid = tile_start < ends[-1]
    probe = jnp.minimum(tile_start, ends[-1] - 1)
    expert = jnp.minimum(jnp.sum((ends[None, :] <= probe[:, None]).astype(I32), axis=1), N_EXPERTS - 1)
    first = jnp.concatenate([jnp.ones((1,), I32), (expert[1:] != expert[:-1]).astype(I32)])
    real_end = jnp.sum((ends - padded + counts)[None, :] * (expert[:, None] == jnp.arange(N_EXPERTS, dtype=I32)[None, :]), axis=1)
    step = tm // MOE_ROW_SPLIT
    rows = jnp.clip(real_end - tile_start, 0, tm)
    rows = jnp.where(valid, ((rows + step - 1) // step) * step, 0)
    return pos.reshape(2, T).astype(I32), src, expert, rows.astype(I32), first


def _row_copy(table_hbm, row, buf, r, sem):
    return pltpu.make_async_copy(table_hbm.at[pl.ds(row, 1)], buf.at[pl.ds(r, 1)], sem)


def _start_rows(index_of, n_rows, table_hbm, buf, sem):
    def start(r, carry):
        _row_copy(table_hbm, index_of(r), buf, r, sem).start()
        return carry
    lax.fori_loop(0, n_rows, start, 0, unroll=DMA_UNROLL)


def _wait_rows(n_rows, table_hbm, buf, sem):
    def wait(r, carry):
        _row_copy(table_hbm, 0, buf, r, sem).wait()
        return carry
    lax.fori_loop(0, n_rows, wait, 0, unroll=DMA_UNROLL)


def _moe_gather_body(src_ref, nxt_ref, x_hbm, g_ref, o_ref, buf, sem):
    i = pl.program_id(0)
    tm = buf.shape[1]
    slot = i % 2

    @pl.when(i == 0)
    def _():
        _start_rows(lambda r: src_ref[0, 0, r], tm, x_hbm, buf.at[0], sem.at[0])

    @pl.when(i + 1 < pl.num_programs(0))
    def _():
        _start_rows(lambda r: nxt_ref[0, 0, r], tm, x_hbm, buf.at[1 - slot], sem.at[1 - slot])

    _wait_rows(tm, x_hbm, buf.at[slot], sem.at[slot])
    o_ref[...] = _rms_rows(buf[slot], g_ref[...]).astype(o_ref.dtype)


def _moe_gather(x, g, src, tm):
    n_tiles = src.shape[0] // tm
    src3 = src.reshape(n_tiles, 1, tm)
    return pl.pallas_call(
        _moe_gather_body, out_shape=jax.ShapeDtypeStruct((n_tiles * tm, D_MODEL), BF16), grid=(n_tiles,),
        in_specs=[pl.BlockSpec((1, 1, tm), lambda i: (i, 0, 0), memory_space=pltpu.SMEM),
                  pl.BlockSpec((1, 1, tm), lambda i: (jnp.minimum(i + 1, n_tiles - 1), 0, 0), memory_space=pltpu.SMEM),
                  pl.BlockSpec(memory_space=pl.ANY), pl.BlockSpec((1, D_MODEL), lambda i: (0, 0))],
        out_specs=pl.BlockSpec((tm, D_MODEL), lambda i: (i, 0)),
        scratch_shapes=[pltpu.VMEM((2, tm, D_MODEL), F32), pltpu.SemaphoreType.DMA((2,))],
        compiler_params=_params(1, 32), name="moe_gather")(src3, src3, x, g.reshape(1, D_MODEL))


def _for_leading_rows(rows, o_ref, compute):
    tm = o_ref.shape[0]
    step = tm // MOE_ROW_SPLIT
    for n in range(0, tm + 1, step):
        @pl.when(rows == n)
        def _(n=n):
            if n:
                o_ref[0:n, :] = compute(n).astype(o_ref.dtype)
            if n < tm:
                o_ref[n:tm, :] = jnp.zeros((tm - n, o_ref.shape[1]), o_ref.dtype)


def _moe_up_body(te_ref, tr_ref, tf_ref, x_ref, wg_ref, wu_ref, o_ref, wg_bf, wu_bf):
    i = pl.program_id(1)

    @pl.when(tf_ref[i] != 0)
    def _():
        wg_bf[...] = wg_ref[...].astype(BF16)
        wu_bf[...] = wu_ref[...].astype(BF16)

    def swiglu(n):
        x = x_ref[0:n, :]
        g = _dot(x, wg_bf[...])
        return g * _sigmoid(g) * _dot(x, wu_bf[...])

    _for_leading_rows(tr_ref[i], o_ref, swiglu)


def _moe_down_body(te_ref, tr_ref, x_ref, w_ref, o_ref):
    i = pl.program_id(1)
    _for_leading_rows(tr_ref[i], o_ref, lambda n: _dot(x_ref[0:n, :], w_ref[...]))


def _moe_experts(xs, w_gate_up, w_down, tile_expert, tile_rows, tile_first, tm):
    P = xs.shape[0]
    n_tiles = P // tm
    tn = _pick(EXPERT_FF, 1024)
    up_blocks = EXPERT_FF // tn
    act = pl.pallas_call(
        _moe_up_body, out_shape=jax.ShapeDtypeStruct((P, EXPERT_FF), BF16),
        grid_spec=pltpu.PrefetchScalarGridSpec(
            num_scalar_prefetch=3, grid=(up_blocks, n_tiles),
            in_specs=[pl.BlockSpec((tm, D_MODEL), lambda j, i, te, tv, tf: (i, 0)),
                      pl.BlockSpec((None, D_MODEL, tn), lambda j, i, te, tv, tf: (te[i], 0, j)),
                      pl.BlockSpec((None, D_MODEL, tn), lambda j, i, te, tv, tf: (te[i], 0, j + up_blocks))],
            out_specs=pl.BlockSpec((tm, tn), lambda j, i, te, tv, tf: (i, j)),
            scratch_shapes=[pltpu.VMEM((D_MODEL, tn), BF16)] * 2),
        compiler_params=_params(2, 60), name="moe_up")(tile_expert, tile_rows, tile_first, xs, w_gate_up, w_gate_up)
    tn = _pick(D_MODEL, 1024)
    return pl.pallas_call(
        _moe_down_body, out_shape=jax.ShapeDtypeStruct((P, D_MODEL), F32),
        grid_spec=pltpu.PrefetchScalarGridSpec(
            num_scalar_prefetch=2, grid=(D_MODEL // tn, n_tiles),
            in_specs=[pl.BlockSpec((tm, EXPERT_FF), lambda j, i, te, tv: (i, 0)),
                      pl.BlockSpec((None, EXPERT_FF, tn), lambda j, i, te, tv: (te[i], 0, j))],
            out_specs=pl.BlockSpec((tm, tn), lambda j, i, te, tv: (i, j))),
        compiler_params=_params(2, 56), name="moe_down")(tile_expert, tile_rows, act, w_down)


def _moe_combine_body(pos_ref, nxt_ref, x_ref, w_ref, ys_hbm, g_ref, xo_ref, ho_ref, buf, sem):
    i = pl.program_id(0)
    tc = buf.shape[2]
    slot = i % 2

    def start(table, s):
        for k in range(2):
            _start_rows(functools.partial(lambda r, k: table[0, k, r], k=k), tc, ys_hbm, buf.at[s, k], sem.at[s])

    @pl.when(i == 0)
    def _():
        start(pos_ref, 0)

    @pl.when(i + 1 < pl.num_programs(0))
    def _():
        start(nxt_ref, 1 - slot)

    for k in range(2):
        _wait_rows(tc, ys_hbm, buf.at[slot, k], sem.at[slot])
    w = w_ref[...]
    x = x_ref[...] + (w[:, 0:1] * buf[slot, 0] + w[:, 1:2] * buf[slot, 1])
    xo_ref[...] = x
    ho_ref[...] = _rms_rows(x, g_ref[...]).astype(ho_ref.dtype)


def _moe_combine(x, ys, pos, w, g):
    T = x.shape[0]
    tc = _pick(T, 256)
    n = T // tc
    pos_t = pos.reshape(2, n, tc).transpose(1, 0, 2)
    return pl.pallas_call(
        _moe_combine_body,
        out_shape=(jax.ShapeDtypeStruct((T, D_MODEL), F32), jax.ShapeDtypeStruct((T, D_MODEL), BF16)),
        grid=(n,),
        in_specs=[pl.BlockSpec((1, 2, tc), lambda i: (i, 0, 0), memory_space=pltpu.SMEM),
                  pl.BlockSpec((1, 2, tc), lambda i: (jnp.minimum(i + 1, n - 1), 0, 0), memory_space=pltpu.SMEM),
                  pl.BlockSpec((tc, D_MODEL), lambda i: (i, 0)), pl.BlockSpec((tc, 2), lambda i: (i, 0)),
                  pl.BlockSpec(memory_space=pl.ANY), pl.BlockSpec((1, D_MODEL), lambda i: (0, 0))],
        out_specs=(pl.BlockSpec((tc, D_MODEL), lambda i: (i, 0)), pl.BlockSpec((tc, D_MODEL), lambda i: (i, 0))),
        scratch_shapes=[pltpu.VMEM((2, 2, tc, D_MODEL), F32), pltpu.SemaphoreType.DMA((2,))],
        compiler_params=_params(1, 40), name="moe_combine")(pos_t, pos_t, x, w.T, ys, g.reshape(1, D_MODEL))


def _rope_tables(pos):
    half = HEAD_DIM // 2
    inv_freq = ROPE_THETA ** (-jnp.arange(half, dtype=F32) * 2.0 / HEAD_DIM)
    ang = pos.astype(F32)[:, None] * inv_freq[None, :]
    cos, sin = jnp.cos(ang), jnp.sin(ang)
    reps = LANES // HEAD_DIM
    return jnp.tile(jnp.concatenate([cos, cos], axis=-1), (1, reps)), jnp.tile(jnp.concatenate([-sin, sin], axis=-1), (1, reps))


def kernel(x_prompt, x_sample, p_prompt, p_sample, cache_conv, state_ssm, cache_k, cache_v, norm_mix, norm_ffn, norm_ple, norm_final, mamba_in_proj, mamba_conv_w, mamba_conv_b, mamba_dt_bias, mamba_a_log, mamba_d, mamba_norm, mamba_out_proj, norm_kv, w_kv, w_q, attn_sinks, w_o, ffn_w_gate_up, ffn_w_down, moe_router, moe_w_gate_up, moe_w_down, ple_proj, ple_gate):
    n_prompt, seq, _ = x_prompt.shape
    n_dec, dec_len, _ = x_sample.shape
    Tp = n_prompt * seq
    T = Tp + n_dec * dec_len
    assert seq % CHUNK == 0 and Tp % dec_len == 0 and dec_len % 8 == 0

    bf = lambda w: w.astype(BF16)
    x0 = jnp.concatenate([x_prompt.reshape(Tp, D_MODEL), x_sample.reshape(-1, D_MODEL)], axis=0)
    p_all = jnp.concatenate([p_prompt.reshape(2, Tp, PLE_DIM), p_sample.reshape(2, -1, PLE_DIM)], axis=1).astype(BF16)
    pos = jnp.concatenate([jnp.tile(jnp.arange(seq, dtype=I32), n_prompt),
                           PAST_LEN + jnp.tile(jnp.arange(dec_len, dtype=I32), n_dec)])
    cos_t, sin_t = _rope_tables(pos)

    tm_wide = _pick(T, 1536)
    tm_mid = _pick(T, 768)
    tm_deep = _pick(T, 512)

    w_in = mamba_in_proj[0]
    h = _rmsnorm(x0, norm_mix[0], BF16)
    tn = _pick(D_INNER, 1024)
    z = _mm_call(_mm_plain_body, T, D_INNER, tm_wide, tn, [("lhs", h), ("w", w_in, 0)], BF16, 56, "in_proj_z")
    xbc = _mm_call(_mm_plain_body, T, CONV_DIM, tm_wide, tn, [("lhs", h), ("w", w_in, D_INNER // tn)], F32, 56, "in_proj_xbc")
    dt = _mm_call(_mm_plain_body, T, SSM_HEADS, tm_wide, SSM_HEADS,
                  [("lhs", h), ("w", w_in[:, D_INNER + CONV_DIM:], 0)], F32, 32, "in_proj_dt")
    ssd_prm = dict(conv_w=mamba_conv_w[0], conv_b=mamba_conv_b[0].reshape(1, CONV_DIM),
                   dt_bias=mamba_dt_bias[0].reshape(1, SSM_HEADS), a_log=mamba_a_log[0].reshape(1, SSM_HEADS),
                   d_full=jnp.repeat(mamba_d[0], SSM_HEAD_DIM).reshape(1, D_INNER),
                   norm_w=mamba_norm[0].reshape(1, D_INNER))
    g, conv_prompt, ssm_prompt = _ssd_call(z, xbc, dt, ssd_prm, n_seq=n_prompt, seq_len=seq, row0=0, Q=CHUNK)
    g, conv_sample, ssm_sample = _ssd_call(z, xbc, dt, ssd_prm, n_seq=n_dec, seq_len=dec_len, row0=Tp, Q=dec_len,
                                           conv0=cache_conv[0], ssm0=state_ssm[0], g_in=g)
    tn = _pick(D_MODEL, 512)
    x1, x1b = _mm_call(_mm_res_body, T, D_MODEL, tm_mid, tn,
                       [("lhs", g), ("w", mamba_out_proj[0], 0), ("row", x0, True)], F32, 48, "out_proj", copy_dtype=BF16)

    tn = _pick(D_FF, 512)
    w_gu = ffn_w_gate_up[0]
    a = _mm_call(functools.partial(_mm_swiglu_body, normed=True), T, D_FF, tm_wide, tn,
                 [("lhs", x1b), ("w", w_gu, 0, norm_ffn[0]), ("w", w_gu, D_FF // tn, norm_ffn[0])], BF16, 56, "ffn_up")
    tn = _pick(D_MODEL, 512)
    x2, x2b = _mm_call(_mm_res_body, T, D_MODEL, tm_deep, tn,
                       [("lhs", a), ("w", ffn_w_down[0], 0), ("row", x1, True)], F32, 52, "ffn_down", copy_dtype=BF16)
    tn = _pick(D_MODEL, 512)
    x3, x3b = _mm_call(functools.partial(_mm_ple_body, normed=True), T, D_MODEL, tm_wide, tn,
                       [("lhs", x2b), ("w", ple_gate[0], 0, norm_ple[0]), ("lhs", p_all[0]), ("w", ple_proj[0], 0),
                        ("row", x2, True)], F32, 48, "ple0", copy_dtype=BF16)

    KW = N_KV_HEADS * HEAD_DIM
    kv = _mm_call(functools.partial(_mm_rope_body, n_rope=KW // LANES, scale=1.0, normed=True), T, 2 * KW, tm_wide, 2 * KW,
                  [("lhs", x3b), ("w", w_kv, 0, norm_kv), ("row", cos_t, False), ("row", sin_t, False)], F32, 48, "kv_proj")
    tn = _pick(D_MODEL, 1024)
    q = _mm_call(functools.partial(_mm_rope_body, n_rope=tn // LANES, scale=HEAD_DIM ** -0.5, normed=True),
                 T, D_MODEL, tm_wide, tn,
                 [("lhs", x3b), ("w", w_q[0], 0, norm_mix[1]), ("row", cos_t, False), ("row", sin_t, False)],
                 BF16, 56, "q_proj")
    o = _attention(q, kv, attn_sinks[0], cache_k, cache_v, n_prompt=n_prompt, seq=seq, n_dec=n_dec, dec_len=dec_len)
    tn = _pick(D_MODEL, 1024)
    x4 = _mm_call(_mm_res_body, T, D_MODEL, tm_mid, tn,
                  [("lhs", o), ("w", w_o[0], 0), ("row", x3, True)], F32, 48, "o_proj")

    tm_moe = _pick(T, 512)
    n_tiles = (2 * T) // tm_moe + N_EXPERTS
    idx, gate_w = _router(x4, norm_ffn[1], moe_router[0])
    pos_sorted, src, tile_expert, tile_rows, tile_first = _route_plan(idx, tm_moe, n_tiles)
    xs = _moe_gather(x4, norm_ffn[1], src, tm_moe)
    ys = _moe_experts(xs, moe_w_gate_up[0], bf(moe_w_down[0]), tile_expert, tile_rows, tile_first, tm_moe)
    x5, h = _moe_combine(x4, ys, pos_sorted, gate_w, norm_ple[1])
    tn = _pick(D_MODEL, 512)
    x6 = _mm_call(functools.partial(_mm_ple_body, normed=False), T, D_MODEL, tm_wide, tn,
                  [("lhs", h), ("w", ple_gate[1], 0), ("lhs", p_all[1]), ("w", ple_proj[1], 0), ("row", x5, True)],
                  F32, 48, "ple1")
    y_prompt, y_sample = _rmsnorm_split(x6, norm_final, Tp)

    k_all = kv[:, :KW]
    v_all = kv[:, KW:]
    tail = lambda a: a[:Tp].reshape(n_prompt, seq, N_KV_HEADS, HEAD_DIM)[:, seq - WINDOW:]
    new = lambda a: a[Tp:].reshape(n_dec, dec_len, N_KV_HEADS, HEAD_DIM)
    k_sample = jnp.concatenate([cache_k, new(k_all)], axis=1)[:, -WINDOW:]
    v_sample = jnp.concatenate([cache_v, new(v_all)], axis=1)[:, -WINDOW:]
    return (y_prompt.reshape(n_prompt, seq, D_MODEL), y_sample.reshape(n_dec, dec_len, D_MODEL),
            conv_prompt[None], ssm_prompt[None], tail(k_all), tail(v_all),
            conv_sample[None], ssm_sample[None], k_sample, v_sample)
```

```python
import functools

import jax
import jax.numpy as jnp
from jax import lax
from jax.experimental import pallas as pl
from jax.experimental.pallas import tpu as pltpu

F32 = jnp.float32
BF16 = jnp.bfloat16
I32 = jnp.int32
HI = lax.Precision.HIGHEST

D_MODEL = 2048
D_INNER = 4096
SSM_HEADS = 64
SSM_HEAD_DIM = 64
SSM_GROUPS = 8
SSM_STATE = 128
CONV_WIDTH = 4
CONV_DIM = D_INNER + 2 * SSM_GROUPS * SSM_STATE
N_Q_HEADS = 32
N_KV_HEADS = 4
Q_PER_KV = N_Q_HEADS // N_KV_HEADS
HEAD_DIM = 64
WINDOW = 128
CHUNK = 64
ROPE_THETA = 10000.0
D_FF = 5632
N_EXPERTS = 8
EXPERT_FF = 7168
PLE_DIM = 256
PAST_LEN = 4096
EPS = 1e-6

LANES = 128
MIB = 1024 * 1024
NT_DIMS = (((1,), (1,)), ((), ()))
DMA_UNROLL = 8
CONV_COLS = 512
PAIR_W = 2 * SSM_HEAD_DIM
MOE_ROW_SPLIT = 4


def _params(n_axes, vmem_mib):
    return pltpu.CompilerParams(dimension_semantics=("arbitrary",) * n_axes,
                                vmem_limit_bytes=vmem_mib * MIB)


def _pick(n, pref):
    for t in (1536, 1024, 768, 512, 384, 256, 128, 96, 64, 32, 16, 8):
        if t <= pref and n % t == 0:
            return t
    raise ValueError(f"no tile for {n}")


def _dot(a, b):
    return jnp.dot(a, b, preferred_element_type=F32)


def _dot_nt(a, b, precision=None):
    return lax.dot_general(a, b, NT_DIMS, precision=precision, preferred_element_type=F32)


def _sigmoid(x):
    return 1.0 / (1.0 + jnp.exp(-x))


def _softplus(x):
    return jnp.maximum(x, 0.0) + jnp.log1p(jnp.exp(-jnp.abs(x)))


def _eye(n, dtype):
    r = lax.broadcasted_iota(I32, (n, n), 0)
    c = lax.broadcasted_iota(I32, (n, n), 1)
    return (r == c).astype(dtype)


def _rms_rows(x, g):
    inv = lax.rsqrt(jnp.mean(x * x, axis=-1, keepdims=True) + EPS)
    return x * inv * g


def _rms_body(x_ref, g_ref, o_ref):
    o_ref[...] = _rms_rows(x_ref[...], g_ref[...]).astype(o_ref.dtype)


def _rmsnorm(x, g, out_dtype):
    T, D = x.shape
    tr = _pick(T, 512)
    return pl.pallas_call(
        _rms_body, out_shape=jax.ShapeDtypeStruct((T, D), out_dtype), grid=(T // tr,),
        in_specs=[pl.BlockSpec((tr, D), lambda i: (i, 0)), pl.BlockSpec((1, D), lambda i: (0, 0))],
        out_specs=pl.BlockSpec((tr, D), lambda i: (i, 0)),
        compiler_params=_params(1, 32), name="rmsnorm")(x, g.reshape(1, D))


def _rms_split_body(x_ref, g_ref, o1_ref, o2_ref, *, n_first):
    i = pl.program_id(0)
    y = _rms_rows(x_ref[...], g_ref[...])

    @pl.when(i < n_first)
    def _():
        o1_ref[...] = y

    @pl.when(i >= n_first)
    def _():
        o2_ref[...] = y


def _rmsnorm_split(x, g, rows_first):
    T, D = x.shape
    rest = T - rows_first
    tr = _pick(rows_first, 512)
    while rest % tr:
        tr //= 2
    n_first = rows_first // tr
    return pl.pallas_call(
        functools.partial(_rms_split_body, n_first=n_first),
        out_shape=(jax.ShapeDtypeStruct((rows_first, D), F32), jax.ShapeDtypeStruct((rest, D), F32)), grid=(T // tr,),
        in_specs=[pl.BlockSpec((tr, D), lambda i: (i, 0)), pl.BlockSpec((1, D), lambda i: (0, 0))],
        out_specs=(pl.BlockSpec((tr, D), lambda i: (jnp.minimum(i, n_first - 1), 0)),
                   pl.BlockSpec((tr, D), lambda i: (jnp.maximum(i - n_first, 0), 0))),
        compiler_params=_params(1, 32), name="rmsnorm_final")(x, g.reshape(1, D))


def _mm_plain_body(x_ref, w_ref, o_ref):
    o_ref[...] = _dot(x_ref[...], w_ref[...]).astype(o_ref.dtype)


def _inv_rms(x_ref):
    x = x_ref[...].astype(F32)
    return lax.rsqrt(jnp.mean(x * x, axis=-1, keepdims=True) + EPS)


def _store_with_copy(y, o_ref, copy_ref):
    o_ref[...] = y
    if copy_ref is not None:
        copy_ref[...] = y.astype(copy_ref.dtype)


def _mm_res_body(x_ref, w_ref, r_ref, o_ref, copy_ref=None):
    _store_with_copy(r_ref[...] + _dot(x_ref[...], w_ref[...]), o_ref, copy_ref)


def _mm_swiglu_body(x_ref, wg_ref, wu_ref, o_ref, *, normed):
    x = x_ref[...]
    g = _dot(x, wg_ref[...])
    u = _dot(x, wu_ref[...])
    if normed:
        inv = _inv_rms(x_ref)
        g, u = g * inv, u * inv
    o_ref[...] = (g * _sigmoid(g) * u).astype(o_ref.dtype)


def _mm_ple_body(h_ref, wg_ref, p_ref, wp_ref, r_ref, o_ref, copy_ref=None, *, normed):
    a = _dot(h_ref[...], wg_ref[...])
    if normed:
        a = a * _inv_rms(h_ref)
    _store_with_copy(r_ref[...] + _sigmoid(a) * _dot(p_ref[...], wp_ref[...]), o_ref, copy_ref)


def _mm_rope_body(x_ref, w_ref, cos_ref, sin_ref, o_ref, *, n_rope, scale, normed):
    acc = _dot(x_ref[...], w_ref[...])
    if normed:
        acc = acc * _inv_rms(x_ref)
    tm, tn = acc.shape
    cos = cos_ref[...]
    sin = sin_ref[...]
    lane = lax.broadcasted_iota(I32, (tm, LANES), 1)
    first_half = (lane % HEAD_DIM) < (HEAD_DIM // 2)
    for c in range(tn // LANES):
        x = acc[:, c * LANES:(c + 1) * LANES]
        if c < n_rope:
            partner = jnp.where(first_half, pltpu.roll(x, LANES - HEAD_DIM // 2, 1),
                                pltpu.roll(x, HEAD_DIM // 2, 1))
            x = (x * cos + partner * sin) * scale
        o_ref[:, c * LANES:(c + 1) * LANES] = x.astype(o_ref.dtype)


def _with_weight_cast(body, w_slots, n_ops, n_in, n_out):
    def wrapped(*refs):
        ins, outs, scratch = list(refs[:n_ops]), refs[n_in:n_in + n_out], refs[n_in + n_out:]

        @pl.when(pl.program_id(1) == 0)
        def _():
            for (k, gk), s in zip(w_slots, scratch):
                w = ins[k][...]
                s[...] = (w if gk is None else w * refs[gk][...]).astype(BF16)

        for (k, _), s in zip(w_slots, scratch):
            ins[k] = s
        body(*ins, *outs)
    return wrapped


def _mm_call(body, T, N, tm, tn, operands, out_dtype, vmem_mib, name, copy_dtype=None):
    in_specs, args, w_slots, scratch, gains = [], [], [], [], []
    for op in operands:
        kind, a = op[0], op[1]
        if kind == "lhs":
            in_specs.append(pl.BlockSpec((tm, a.shape[1]), lambda j, i: (i, 0)))
        elif kind == "w":
            in_specs.append(pl.BlockSpec((a.shape[0], tn), functools.partial(lambda j, i, off: (0, j + off), off=op[2])))
            w_slots.append([len(args), None])
            if len(op) > 3:
                w_slots[-1][1] = len(operands) + len(gains)
                gains.append(op[3].reshape(-1, 1))
            scratch.append(pltpu.VMEM((a.shape[0], tn), BF16))
        elif op[2]:
            in_specs.append(pl.BlockSpec((tm, tn), lambda j, i: (i, j)))
        else:
            in_specs.append(pl.BlockSpec((tm, a.shape[1]), lambda j, i: (i, 0)))
        args.append(a)
    for g in gains:
        in_specs.append(pl.BlockSpec(g.shape, lambda j, i: (0, 0)))
        args.append(g)
    tile = pl.BlockSpec((tm, tn), lambda j, i: (i, j))
    out_shape = jax.ShapeDtypeStruct((T, N), out_dtype)
    n_out = 1
    if copy_dtype is not None:
        out_shape, tile, n_out = (out_shape, jax.ShapeDtypeStruct((T, N), copy_dtype)), (tile, tile), 2
    return pl.pallas_call(
        _with_weight_cast(body, w_slots, len(operands), len(args), n_out), out_shape=out_shape,
        grid=(N // tn, T // tm), in_specs=in_specs, out_specs=tile,
        scratch_shapes=scratch, compiler_params=_params(2, vmem_mib), name=name)(*args)


def _ssd_body(*refs, Q, nc, has_init):
    if has_init:
        (z_ref, xbc_ref, dt_ref, cw_ref, cb_ref, dtb_ref, alog_ref, dfull_ref, nw_ref, conv0_ref, ssm0_ref,
         _, g_ref, convo_ref, ssmo_ref, xp, HT, ysc, act, xsb) = refs
    else:
        (z_ref, xbc_ref, dt_ref, cw_ref, cb_ref, dtb_ref, alog_ref, dfull_ref, nw_ref,
         g_ref, convo_ref, ssmo_ref, xp, HT, ysc, act, xsb) = refs
    c = pl.program_id(1)
    HPG = SSM_HEADS // SSM_GROUPS
    eye_n_bf = _eye(SSM_STATE, BF16)

    @pl.when(c == 0)
    def _init():
        xp[0:8, :] = jnp.zeros((8, CONV_DIM), F32)
        if has_init:
            xp[8 - (CONV_WIDTH - 1):8, :] = conv0_ref[...]
            for g in range(SSM_GROUPS):
                HT[g] = ssm0_ref[g * HPG:(g + 1) * HPG].reshape(HPG * SSM_HEAD_DIM, SSM_STATE).T
        else:
            HT[...] = jnp.zeros(HT.shape, F32)

    xp[8:8 + Q, :] = xbc_ref[...]
    for k in range(CONV_DIM // CONV_COLS):
        cs = slice(k * CONV_COLS, (k + 1) * CONV_COLS)
        xe = xp[:, cs]
        cw = cw_ref[:, cs]
        conv = cb_ref[:, cs]
        for t in range(CONV_WIDTH - 1):
            conv = conv + pltpu.roll(xe, CONV_WIDTH - 1 - t, 0)[8:, :] * cw[t:t + 1, :]
        conv = conv + xe[8:, :] * cw[CONV_WIDTH - 1:CONV_WIDTH, :]
        a = conv * _sigmoid(conv)
        act[:, cs] = a
        if k * CONV_COLS < D_INNER:
            xsb[:, cs] = a.astype(BF16)

    @pl.when(c == nc - 1)
    def _conv_out():
        convo_ref[...] = xp[8 + Q - (CONV_WIDTH - 1):8 + Q, :]

    xp[0:8, :] = xp[Q:Q + 8, :]

    dt = _softplus(dt_ref[...] + dtb_ref[...])
    dA = dt * (-jnp.exp(alog_ref[...]))
    tril = lax.broadcasted_iota(I32, (Q, Q), 0) >= lax.broadcasted_iota(I32, (Q, Q), 1)
    a_cs = jnp.dot(tril.astype(F32), dA, precision=HI, preferred_element_type=F32)
    a_last = a_cs[Q - 1:Q, :]
    w_end = jnp.exp(a_last - a_cs) * dt
    cdec = jnp.exp(a_last)

    n_pairs = SSM_HEADS // 2
    pr = lax.broadcasted_iota(I32, (n_pairs, SSM_HEADS), 0)
    pc_ = lax.broadcasted_iota(I32, (n_pairs, SSM_HEADS), 1)
    sel_a = (pc_ == 2 * pr).astype(F32)
    sel_b = (pc_ == 2 * pr + 1).astype(F32)

    def pair_rows(m):
        return jnp.concatenate([_dot_nt(sel_a, m, HI), _dot_nt(sel_b, m, HI)], axis=1)

    a_csT2, dtT2, w_endT2 = pair_rows(a_cs), pair_rows(dt), pair_rows(w_end)
    tok_r =lax.broadcasted_iota(I32, (Q, 2 * Q), 0)
    tok_c = lax.broadcasted_iota(I32, (Q, 2 * Q), 1)
    first_tok = tok_c < Q
    tril2 = tok_r >= jnp.where(first_tok, tok_c, tok_c - Q)
    first_ch = lax.broadcasted_iota(I32, (Q, PAIR_W), 1) < SSM_HEAD_DIM
    first_ch2 = lax.broadcasted_iota(I32, (2 * Q, PAIR_W), 1) < SSM_HEAD_DIM
    first_row = lax.broadcasted_iota(I32, (2 * Q, PAIR_W), 0) < Q
    ssq = jnp.zeros((Q, PAIR_W), F32)

    for g in range(SSM_GROUPS):
        b0 = D_INNER + g * SSM_STATE
        c0 = D_INNER + SSM_GROUPS * SSM_STATE + g * SSM_STATE
        Bg = act[:, b0:b0 + SSM_STATE].astype(BF16)
        Cg = act[:, c0:c0 + SSM_STATE].astype(BF16)
        cb = _dot_nt(Cg, Bg)
        cb2 = jnp.concatenate([cb, cb], axis=1)
        BT = _dot_nt(eye_n_bf, Bg)
        BT2 = jnp.concatenate([BT, BT], axis=1)
        Hg = HT[g]
        yoff = _dot(Cg, Hg.astype(BF16))
        for pp in range(HPG // 2):
            i = g * (HPG // 2) + pp
            ha, hb = 2 * i, 2 * i + 1
            pc = slice(i * PAIR_W, (i + 1) * PAIR_W)
            gc = slice(pp * PAIR_W, (pp + 1) * PAIR_W)
            col_a, col_b = a_cs[:, ha:ha + 1], a_cs[:, hb:hb + 1]
            seg = jnp.where(first_tok, col_a, col_b) - a_csT2[i:i + 1, :]
            dec = jnp.where(tril2, jnp.exp(seg), 0.0)
            M2 = (cb2 * dec * dtT2[i:i + 1, :]).astype(BF16)
            BTw2 = (BT2 * w_endT2[i:i + 1, :]).astype(BF16)
            x2 = xsb[:, pc]
            x2 = jnp.concatenate([x2, x2], axis=0)
            xbd = jnp.where(first_row == first_ch2, x2, jnp.zeros_like(x2))
            r = _dot(jnp.concatenate([M2, BTw2], axis=0), xbd)
            ea2 = jnp.exp(jnp.where(first_ch, col_a, col_b))
            y = r[:Q] + yoff[:, gc] * ea2 + act[:, pc] * dfull_ref[:, pc]
            zf = z_ref[:, pc].astype(F32)
            y = y * (zf * _sigmoid(zf))
            ysc[:, pc] = y
            ssq = ssq + y * y
            cd2 = jnp.where(first_ch[0:1, :], cdec[:, ha:ha + 1], cdec[:, hb:hb + 1])
            HT[g, :, gc] = Hg[:, gc] * cd2 + r[Q:]

    inv = lax.rsqrt(jnp.sum(ssq, axis=-1, keepdims=True) * (1.0 / D_INNER) + EPS)
    g_ref[...] = (ysc[...] * inv * nw_ref[...]).astype(g_ref.dtype)

    @pl.when(c == nc - 1)
    def _state_out():
        for g in range(SSM_GROUPS):
            ssmo_ref[g * HPG:(g + 1) * HPG] = HT[g].T.reshape(HPG, SSM_HEAD_DIM, SSM_STATE)


def _ssd_call(z, xbc, dt, prm, *, n_seq, seq_len, row0, Q, conv0=None, ssm0=None, g_in=None):
    T = z.shape[0]
    nc = seq_len // Q
    blk0 = row0 // Q
    has_init = conv0 is not None
    rows = lambda b, c: (blk0 + b * nc + c, 0)
    const = lambda b, c: (0, 0)
    in_specs = [pl.BlockSpec((Q, D_INNER), rows), pl.BlockSpec((Q, CONV_DIM), rows), pl.BlockSpec((Q, SSM_HEADS), rows),
                pl.BlockSpec((CONV_WIDTH, CONV_DIM), const), pl.BlockSpec((1, CONV_DIM), const),
                pl.BlockSpec((1, SSM_HEADS), const), pl.BlockSpec((1, SSM_HEADS), const),
                pl.BlockSpec((1, D_INNER), const), pl.BlockSpec((1, D_INNER), const)]
    args = [z, xbc, dt, prm["conv_w"], prm["conv_b"], prm["dt_bias"], prm["a_log"], prm["d_full"], prm["norm_w"]]
    aliases = {}
    if has_init:
        in_specs += [pl.BlockSpec((None, CONV_WIDTH - 1, CONV_DIM), lambda b, c: (b, 0, 0)),
                     pl.BlockSpec((None, SSM_HEADS, SSM_HEAD_DIM, SSM_STATE), lambda b, c: (b, 0, 0, 0)),
                     pl.BlockSpec(memory_space=pl.ANY)]
        args += [conv0, ssm0, g_in]
        aliases = {len(args) - 1: 0}
    out_shape = (jax.ShapeDtypeStruct((T, D_INNER), BF16),
                 jax.ShapeDtypeStruct((n_seq, CONV_WIDTH - 1, CONV_DIM), F32),
                 jax.ShapeDtypeStruct((n_seq, SSM_HEADS, SSM_HEAD_DIM, SSM_STATE), F32))
    out_specs = (pl.BlockSpec((Q, D_INNER), rows),
                 pl.BlockSpec((None, CONV_WIDTH - 1, CONV_DIM), lambda b, c: (b, 0, 0)),
                 pl.BlockSpec((None, SSM_HEADS, SSM_HEAD_DIM, SSM_STATE), lambda b, c: (b, 0, 0, 0)))
    scratch = [pltpu.VMEM((Q + 8, CONV_DIM), F32),
               pltpu.VMEM((SSM_GROUPS, SSM_STATE, D_INNER // SSM_GROUPS), F32),
               pltpu.VMEM((Q, D_INNER), F32), pltpu.VMEM((Q, CONV_DIM), F32), pltpu.VMEM((Q, D_INNER), BF16)]
    return pl.pallas_call(
        functools.partial(_ssd_body, Q=Q, nc=nc, has_init=has_init), out_shape=out_shape,
        grid=(n_seq, nc), in_specs=in_specs, out_specs=out_specs, scratch_shapes=scratch,
        input_output_aliases=aliases, compiler_params=_params(2, 40),
        name="ssd_sample" if has_init else "ssd_prompt")(*args)


def _attend(q_ref, sink_ref, o_ref, kband, vband, bias):
    ones = jnp.ones((kband.shape[1], HEAD_DIM), BF16)
    for h in range(N_Q_HEADS):
        kh = h // Q_PER_KV
        hd = slice(h * HEAD_DIM, (h + 1) * HEAD_DIM)
        s = _dot_nt(q_ref[:, hd], kband[kh])
        if bias is not None:
            s = s + bias
        sink = sink_ref[h]
        m = jnp.maximum(jnp.max(s, axis=-1, keepdims=True), sink)
        e = jnp.exp(s - m).astype(BF16)
        den = _dot(e, ones) + jnp.exp(sink - m)
        o_ref[:, hd] = (_dot(e, vband[kh]) / den).astype(o_ref.dtype)


def _stage_kv(kband, vband, row0, k_of, v_of):
    n = k_of(0).shape[0]
    for kh in range(N_KV_HEADS):
        kband[kh, row0:row0 + n, :] = k_of(kh).astype(BF16)
        vband[kh, row0:row0 + n, :] = v_of(kh).astype(BF16)


def _attn_prompt_body(sink_ref, q_ref, kv0_ref, kv1_ref, kv2_ref, o_ref, kband, vband):
    c = pl.program_id(1)
    KW = N_KV_HEADS * HEAD_DIM
    for j, ref in enumerate((kv0_ref, kv1_ref, kv2_ref)):
        _stage_kv(kband, vband, j * CHUNK,
                  functools.partial(lambda kh, r: r[:, kh * HEAD_DIM:(kh + 1) * HEAD_DIM], r=ref),
                  functools.partial(lambda kh, r: r[:, KW + kh * HEAD_DIM:KW + (kh + 1) * HEAD_DIM], r=ref))
    key = lax.broadcasted_iota(I32, (1, kband.shape[1]), 1)
    bias = jnp.where(key >= jnp.maximum(2 - c, 0) * CHUNK, 0.0, -jnp.inf).astype(F32)
    _attend(q_ref, sink_ref, o_ref, kband, vband, bias)


def _attn_sample_body(sink_ref, q_ref, ck_ref, cv_ref, kvn_ref, _, o_ref, kband, vband):
    KW = N_KV_HEADS * HEAD_DIM
    _stage_kv(kband, vband, 0, lambda kh: ck_ref[:, kh * HEAD_DIM:(kh + 1) * HEAD_DIM],
              lambda kh: cv_ref[:, kh * HEAD_DIM:(kh + 1) * HEAD_DIM])
    _stage_kv(kband, vband, WINDOW, lambda kh: kvn_ref[:, kh * HEAD_DIM:(kh + 1) * HEAD_DIM],
              lambda kh: kvn_ref[:, KW + kh * HEAD_DIM:KW + (kh + 1) * HEAD_DIM])
    _attend(q_ref, sink_ref, o_ref, kband, vband, None)


def _attention(q, kv, sinks, cache_k, cache_v, *, n_prompt, seq, n_dec, dec_len):
    T = q.shape[0]
    NQ = N_Q_HEADS * HEAD_DIM
    KW = N_KV_HEADS * HEAD_DIM
    nc = seq // CHUNK
    smem = pl.BlockSpec(memory_space=pltpu.SMEM)
    band = lambda back: (lambda b, c: (b * nc + jnp.maximum(c - back, 0), 0))
    bands = lambda keys: [pltpu.VMEM((N_KV_HEADS, keys, HEAD_DIM), BF16)] * 2
    o = pl.pallas_call(
        _attn_prompt_body, out_shape=jax.ShapeDtypeStruct((T, NQ), BF16), grid=(n_prompt, nc),
        in_specs=[smem, pl.BlockSpec((CHUNK, NQ), lambda b, c: (b * nc + c, 0)),
                  pl.BlockSpec((CHUNK, 2 * KW), band(2)), pl.BlockSpec((CHUNK, 2 * KW), band(1)),
                  pl.BlockSpec((CHUNK, 2 * KW), band(0))],
        out_specs=pl.BlockSpec((CHUNK, NQ), lambda b, c: (b * nc + c, 0)),
        scratch_shapes=bands(WINDOW + CHUNK),
        compiler_params=_params(2, 32), name="attn_prompt")(sinks, q, kv, kv, kv)
    blk0 = (n_prompt * seq) // dec_len
    return pl.pallas_call(
        _attn_sample_body, out_shape=jax.ShapeDtypeStruct((T, NQ), BF16), grid=(n_dec,),
        in_specs=[smem, pl.BlockSpec((dec_len, NQ), lambda b: (blk0 + b, 0)),
                  pl.BlockSpec((None, WINDOW, KW), lambda b: (b, 0, 0)),
                  pl.BlockSpec((None, WINDOW, KW), lambda b: (b, 0, 0)),
                  pl.BlockSpec((dec_len, 2 * KW), lambda b: (blk0 + b, 0)),
                  pl.BlockSpec(memory_space=pl.ANY)],
        out_specs=pl.BlockSpec((dec_len, NQ), lambda b: (blk0 + b, 0)),
        scratch_shapes=bands(WINDOW + dec_len),
        input_output_aliases={5: 0}, compiler_params=_params(1, 32), name="attn_sample")(
            sinks, q, cache_k.reshape(n_dec, WINDOW, KW), cache_v.reshape(n_dec, WINDOW, KW), kv, o)


def _router_body(x_ref, g_ref, rt_ref, idx_ref, w_ref):
    h = _rms_rows(x_ref[...], g_ref[...])
    lt = _dot_nt(rt_ref[...], h, HI)
    ids = lax.broadcasted_iota(I32, lt.shape, 0)
    m1 = jnp.max(lt, axis=0, keepdims=True)
    i1 = jnp.min(jnp.where(lt == m1, ids, N_EXPERTS), axis=0, keepdims=True)
    rest = jnp.where(ids == i1, -jnp.inf, lt)
    m2 = jnp.max(rest, axis=0, keepdims=True)
    i2 = jnp.min(jnp.where(rest == m2, ids, N_EXPERTS), axis=0, keepdims=True)
    e2 = jnp.exp(m2 - m1)
    w1 = 1.0 / (1.0 + e2)
    idx_ref[...] = jnp.concatenate([i1, i2], axis=0)
    w_ref[...] = jnp.concatenate([w1, e2 * w1], axis=0)


def _router(x, g, router):
    T = x.shape[0]
    tm = _pick(T, 512)
    return pl.pallas_call(
        _router_body, out_shape=(jax.ShapeDtypeStruct((2, T), I32), jax.ShapeDtypeStruct((2, T), F32)),
        grid=(T // tm,),
        in_specs=[pl.BlockSpec((tm, D_MODEL), lambda i: (i, 0)), pl.BlockSpec((1, D_MODEL), lambda i: (0, 0)),
                  pl.BlockSpec((N_EXPERTS, D_MODEL), lambda i: (0, 0))],
        out_specs=(pl.BlockSpec((2, tm), lambda i: (0, i)), pl.BlockSpec((2, tm), lambda i: (0, i))),
        compiler_params=_params(1, 32), name="moe_router")(x, g.reshape(1, D_MODEL), router.T)


def _route_plan(idx, tm, n_tiles):
    T = idx.shape[1]
    e_flat = idx.reshape(-1)
    onehot = (e_flat[:, None] == jnp.arange(N_EXPERTS, dtype=I32)[None, :]).astype(I32)
    csum = jnp.cumsum(onehot, axis=0)
    counts = csum[-1]
    rank = jnp.sum(csum * onehot, axis=1) - 1
    padded = ((counts + tm - 1) // tm) * tm
    ends = jnp.cumsum(padded)
    pos = jnp.sum((ends - padded)[None, :] * onehot, axis=1) + rank
    tok = jnp.tile(jnp.arange(T, dtype=I32), 2)
    src = jnp.zeros((n_tiles * tm,), I32).at[pos].set(tok)
    tile_start = jnp.arange(n_tiles, dtype=I32) * tm
    valid = tile_start < ends[-1]
    probe = jnp.minimum(tile_start, ends[-1] - 1)
    expert = jnp.minimum(jnp.sum((ends[None, :] <= probe[:, None]).astype(I32), axis=1), N_EXPERTS - 1)
    first = jnp.concatenate([jnp.ones((1,), I32), (expert[1:] != expert[:-1]).astype(I32)])
    real_end = jnp.sum((ends - padded + counts)[None, :] * (expert[:, None] == jnp.arange(N_EXPERTS, dtype=I32)[None, :]), axis=1)
    step = tm // MOE_ROW_SPLIT
    rows = jnp.clip(real_end - tile_start, 0, tm)
    rows = jnp.where(valid, ((rows + step - 1) // step) * step, 0)
    return pos.reshape(2, T).astype(I32), src, expert, rows.astype(I32), first


def _row_copy(table_hbm, row, buf, r, sem):
    return pltpu.make_async_copy(table_hbm.at[pl.ds(row, 1)], buf.at[pl.ds(r, 1)], sem)


def _start_rows(index_of, n_rows, table_hbm, buf, sem):
    def start(r, carry):
        _row_copy(table_hbm, index_of(r), buf, r, sem).start()
        return carry
    lax.fori_loop(0, n_rows, start, 0, unroll=DMA_UNROLL)


def _wait_rows(n_rows, table_hbm, buf, sem):
    def wait(r, carry):
        _row_copy(table_hbm, 0, buf, r, sem).wait()
        return carry
    lax.fori_loop(0, n_rows, wait, 0, unroll=DMA_UNROLL)


def _moe_gather_body(src_ref, nxt_ref, x_hbm, g_ref, o_ref, buf, sem):
    i = pl.program_id(0)
    tm = buf.shape[1]
    slot = i % 2

    @pl.when(i == 0)
    def _():
        _start_rows(lambda r: src_ref[0, 0, r], tm, x_hbm, buf.at[0], sem.at[0])

    @pl.when(i + 1 < pl.num_programs(0))
    def _():
        _start_rows(lambda r: nxt_ref[0, 0, r], tm, x_hbm, buf.at[1 - slot], sem.at[1 - slot])

    _wait_rows(tm, x_hbm, buf.at[slot], sem.at[slot])
    o_ref[...] = _rms_rows(buf[slot], g_ref[...]).astype(o_ref.dtype)


def _moe_gather(x, g, src, tm):
    n_tiles = src.shape[0] // tm
    src3 = src.reshape(n_tiles, 1, tm)
    return pl.pallas_call(
        _moe_gather_body, out_shape=jax.ShapeDtypeStruct((n_tiles * tm, D_MODEL), BF16), grid=(n_tiles,),
        in_specs=[pl.BlockSpec((1, 1, tm), lambda i: (i, 0, 0), memory_space=pltpu.SMEM),
                  pl.BlockSpec((1, 1, tm), lambda i: (jnp.minimum(i + 1, n_tiles - 1), 0, 0), memory_space=pltpu.SMEM),
                  pl.BlockSpec(memory_space=pl.ANY), pl.BlockSpec((1, D_MODEL), lambda i: (0, 0))],
        out_specs=pl.BlockSpec((tm, D_MODEL), lambda i: (i, 0)),
        scratch_shapes=[pltpu.VMEM((2, tm, D_MODEL), F32), pltpu.SemaphoreType.DMA((2,))],
        compiler_params=_params(1, 32), name="moe_gather")(src3, src3, x, g.reshape(1, D_MODEL))


def _for_leading_rows(rows, o_ref, compute):
    tm = o_ref.shape[0]
    step = tm // MOE_ROW_SPLIT
    for n in range(0, tm + 1, step):
        @pl.when(rows == n)
        def _(n=n):
            if n:
                o_ref[0:n, :] = compute(n).astype(o_ref.dtype)
            if n < tm:
                o_ref[n:tm, :] = jnp.zeros((tm - n, o_ref.shape[1]), o_ref.dtype)


def _moe_up_body(te_ref, tr_ref, tf_ref, x_ref, wg_ref, wu_ref, wd_ref, o_ref, wdo_ref, wg_bf, wu_bf):
    i = pl.program_id(1)
    wdo_ref[...] = wd_ref[...].astype(BF16)

    @pl.when(tf_ref[i] != 0)
    def _():
        wg_bf[...] = wg_ref[...].astype(BF16)
        wu_bf[...] = wu_ref[...].astype(BF16)

    def swiglu(n):
        x = x_ref[0:n, :]
        g = _dot(x, wg_bf[...])
        return g * _sigmoid(g) * _dot(x, wu_bf[...])

    _for_leading_rows(tr_ref[i], o_ref, swiglu)


def _moe_down_body(te_ref, tr_ref, x_ref, w_ref, o_ref):
    i = pl.program_id(1)
    _for_leading_rows(tr_ref[i], o_ref, lambda n: _dot(x_ref[0:n, :], w_ref[...]))


def _moe_experts(xs, w_gate_up, w_down, tile_expert, tile_rows, tile_first, tm):
    P = xs.shape[0]
    n_tiles = P // tm
    tn = _pick(EXPERT_FF, 1024)
    up_blocks = EXPERT_FF // tn
    steps = up_blocks * n_tiles
    wd_rows = N_EXPERTS * EXPERT_FF
    cast_rows = min(d for d in range(16, wd_rows + 1, 16) if wd_rows % d == 0 and d * steps >= wd_rows)
    cast_block = lambda j, i, te, tv, tf: (jnp.minimum(j * n_tiles + i, wd_rows // cast_rows - 1), 0)
    act, w_down = pl.pallas_call(
        _moe_up_body,
        out_shape=(jax.ShapeDtypeStruct((P, EXPERT_FF), BF16), jax.ShapeDtypeStruct((wd_rows, D_MODEL), BF16)),
        grid_spec=pltpu.PrefetchScalarGridSpec(
            num_scalar_prefetch=3, grid=(up_blocks, n_tiles),
            in_specs=[pl.BlockSpec((tm, D_MODEL), lambda j, i, te, tv, tf: (i, 0)),
                      pl.BlockSpec((None, D_MODEL, tn), lambda j, i, te, tv, tf: (te[i], 0, j)),
                      pl.BlockSpec((None, D_MODEL, tn), lambda j, i, te, tv, tf: (te[i], 0, j + up_blocks)),
                      pl.BlockSpec((cast_rows, D_MODEL), cast_block)],
            out_specs=(pl.BlockSpec((tm, tn), lambda j, i, te, tv, tf: (i, j)),
                       pl.BlockSpec((cast_rows, D_MODEL), cast_block)),
            scratch_shapes=[pltpu.VMEM((D_MODEL, tn), BF16)] * 2),
        compiler_params=_params(2, 60), name="moe_up")(
            tile_expert, tile_rows, tile_first, xs, w_gate_up, w_gate_up, w_down.reshape(wd_rows, D_MODEL))
    w_down = w_down.reshape(N_EXPERTS, EXPERT_FF, D_MODEL)
    tn = _pick(D_MODEL, 1024)
    return pl.pallas_call(
        _moe_down_body, out_shape=jax.ShapeDtypeStruct((P, D_MODEL), F32),
        grid_spec=pltpu.PrefetchScalarGridSpec(
            num_scalar_prefetch=2, grid=(D_MODEL // tn, n_tiles),
            in_specs=[pl.BlockSpec((tm, EXPERT_FF), lambda j, i, te, tv: (i, 0)),
                      pl.BlockSpec((None, EXPERT_FF, tn), lambda j, i, te, tv: (te[i], 0, j))],
            out_specs=pl.BlockSpec((tm, tn), lambda j, i, te, tv: (i, j))),
        compiler_params=_params(2, 56), name="moe_down")(tile_expert, tile_rows, act, w_down)


def _moe_combine_body(pos_ref, nxt_ref, x_ref, w_ref, ys_hbm, g_ref, xo_ref, ho_ref, buf, sem):
    i = pl.program_id(0)
    tc = buf.shape[2]
    slot = i % 2

    def start(table, s):
        for k in range(2):
            _start_rows(functools.partial(lambda r, k: table[0, k, r], k=k), tc, ys_hbm, buf.at[s, k], sem.at[s])

    @pl.when(i == 0)
    def _():
        start(pos_ref, 0)

    @pl.when(i + 1 < pl.num_programs(0))
    def _():
        start(nxt_ref, 1 - slot)

    for k in range(2):
        _wait_rows(tc, ys_hbm, buf.at[slot, k], sem.at[slot])
    w = w_ref[...]
    x = x_ref[...] + (w[:, 0:1] * buf[slot, 0] + w[:, 1:2] * buf[slot, 1])
    xo_ref[...] = x
    ho_ref[...] = _rms_rows(x, g_ref[...]).astype(ho_ref.dtype)


def _moe_combine(x, ys, pos, w, g):
    T = x.shape[0]
    tc = _pick(T, 256)
    n = T // tc
    pos_t = pos.reshape(2, n, tc).transpose(1, 0, 2)
    return pl.pallas_call(
        _moe_combine_body,
        out_shape=(jax.ShapeDtypeStruct((T, D_MODEL), F32), jax.ShapeDtypeStruct((T, D_MODEL), BF16)),
        grid=(n,),
        in_specs=[pl.BlockSpec((1, 2, tc), lambda i: (i, 0, 0), memory_space=pltpu.SMEM),
                  pl.BlockSpec((1, 2, tc), lambda i: (jnp.minimum(i + 1, n - 1), 0, 0), memory_space=pltpu.SMEM),
                  pl.BlockSpec((tc, D_MODEL), lambda i: (i, 0)), pl.BlockSpec((tc, 2), lambda i: (i, 0)),
                  pl.BlockSpec(memory_space=pl.ANY), pl.BlockSpec((1, D_MODEL), lambda i: (0, 0))],
        out_specs=(pl.BlockSpec((tc, D_MODEL), lambda i: (i, 0)), pl.BlockSpec((tc, D_MODEL), lambda i: (i, 0))),
        scratch_shapes=[pltpu.VMEM((2, 2, tc, D_MODEL), F32), pltpu.SemaphoreType.DMA((2,))],
        compiler_params=_params(1, 40), name="moe_combine")(pos_t, pos_t, x, w.T, ys, g.reshape(1, D_MODEL))


def _rope_tables(pos):
    half = HEAD_DIM // 2
    inv_freq = ROPE_THETA ** (-jnp.arange(half, dtype=F32) * 2.0 / HEAD_DIM)
    ang = pos.astype(F32)[:, None] * inv_freq[None, :]
    cos, sin = jnp.cos(ang), jnp.sin(ang)
    reps = LANES // HEAD_DIM
    return jnp.tile(jnp.concatenate([cos, cos], axis=-1), (1, reps)), jnp.tile(jnp.concatenate([-sin, sin], axis=-1), (1, reps))


def kernel(x_prompt, x_sample, p_prompt, p_sample, cache_conv, state_ssm, cache_k, cache_v, norm_mix, norm_ffn, norm_ple, norm_final, mamba_in_proj, mamba_conv_w, mamba_conv_b, mamba_dt_bias, mamba_a_log, mamba_d, mamba_norm, mamba_out_proj, norm_kv, w_kv, w_q, attn_sinks, w_o, ffn_w_gate_up, ffn_w_down, moe_router, moe_w_gate_up, moe_w_down, ple_proj, ple_gate):
    n_prompt, seq, _ = x_prompt.shape
    n_dec, dec_len, _ = x_sample.shape
    Tp = n_prompt * seq
    T = Tp + n_dec * dec_len
    assert seq % CHUNK == 0 and Tp % dec_len == 0 and dec_len % 8 == 0

    x0 = jnp.concatenate([x_prompt.reshape(Tp, D_MODEL), x_sample.reshape(-1, D_MODEL)], axis=0)
    p_all = jnp.concatenate([p_prompt.reshape(2, Tp, PLE_DIM), p_sample.reshape(2, -1, PLE_DIM)], axis=1).astype(BF16)
    pos = jnp.concatenate([jnp.tile(jnp.arange(seq, dtype=I32), n_prompt),
                           PAST_LEN + jnp.tile(jnp.arange(dec_len, dtype=I32), n_dec)])
    cos_t, sin_t = _rope_tables(pos)

    tm_wide = _pick(T, 1536)
    tm_mid = _pick(T, 768)
    tm_deep = _pick(T, 512)

    w_in = mamba_in_proj[0]
    h = _rmsnorm(x0, norm_mix[0], BF16)
    tn = _pick(D_INNER, 1024)
    z = _mm_call(_mm_plain_body, T, D_INNER, tm_wide, tn, [("lhs", h), ("w", w_in, 0)], BF16, 56, "in_proj_z")
    xbc = _mm_call(_mm_plain_body, T, CONV_DIM, tm_wide, tn, [("lhs", h), ("w", w_in, D_INNER // tn)], F32, 56, "in_proj_xbc")
    dt = _mm_call(_mm_plain_body, T, SSM_HEADS, tm_wide, SSM_HEADS,
                  [("lhs", h), ("w", w_in[:, D_INNER + CONV_DIM:], 0)], F32, 32, "in_proj_dt")
    ssd_prm = dict(conv_w=mamba_conv_w[0], conv_b=mamba_conv_b[0].reshape(1, CONV_DIM),
                   dt_bias=mamba_dt_bias[0].reshape(1, SSM_HEADS), a_log=mamba_a_log[0].reshape(1, SSM_HEADS),
                   d_full=jnp.repeat(mamba_d[0], SSM_HEAD_DIM).reshape(1, D_INNER),
                   norm_w=mamba_norm[0].reshape(1, D_INNER))
    g, conv_prompt, ssm_prompt = _ssd_call(z, xbc, dt, ssd_prm, n_seq=n_prompt, seq_len=seq, row0=0, Q=CHUNK)
    g, conv_sample, ssm_sample = _ssd_call(z, xbc, dt, ssd_prm, n_seq=n_dec, seq_len=dec_len, row0=Tp, Q=dec_len,
                                           conv0=cache_conv[0], ssm0=state_ssm[0], g_in=g)
    tn = _pick(D_MODEL, 512)
    x1, x1b = _mm_call(_mm_res_body, T, D_MODEL, tm_mid, tn,
                       [("lhs", g), ("w", mamba_out_proj[0], 0), ("row", x0, True)], F32, 48, "out_proj", copy_dtype=BF16)

    tn = _pick(D_FF, 512)
    w_gu = ffn_w_gate_up[0]
    a = _mm_call(functools.partial(_mm_swiglu_body, normed=True), T, D_FF, tm_wide, tn,
                 [("lhs", x1b), ("w", w_gu, 0, norm_ffn[0]), ("w", w_gu, D_FF // tn, norm_ffn[0])], BF16, 56, "ffn_up")
    tn = _pick(D_MODEL, 512)
    x2, x2b = _mm_call(_mm_res_body, T, D_MODEL, tm_deep, tn,
                       [("lhs", a), ("w", ffn_w_down[0], 0), ("row", x1, True)], F32, 52, "ffn_down", copy_dtype=BF16)
    tn = _pick(D_MODEL, 512)
    x3, x3b = _mm_call(functools.partial(_mm_ple_body, normed=True), T, D_MODEL, tm_wide, tn,
                       [("lhs", x2b), ("w", ple_gate[0], 0, norm_ple[0]), ("lhs", p_all[0]), ("w", ple_proj[0], 0),
                        ("row", x2, True)], F32, 48, "ple0", copy_dtype=BF16)

    KW = N_KV_HEADS * HEAD_DIM
    kv = _mm_call(functools.partial(_mm_rope_body, n_rope=KW // LANES, scale=1.0, normed=True), T, 2 * KW, tm_wide, 2 * KW,
                  [("lhs", x3b), ("w", w_kv, 0, norm_kv), ("row", cos_t, False), ("row", sin_t, False)], F32, 48, "kv_proj")
    tn = _pick(D_MODEL, 1024)
    q = _mm_call(functools.partial(_mm_rope_body, n_rope=tn // LANES, scale=HEAD_DIM ** -0.5, normed=True),
                 T, D_MODEL, tm_wide, tn,
                 [("lhs", x3b), ("w", w_q[0], 0, norm_mix[1]), ("row", cos_t, False), ("row", sin_t, False)],
                 BF16, 56, "q_proj")
    o = _attention(q, kv, attn_sinks[0], cache_k, cache_v, n_prompt=n_prompt, seq=seq, n_dec=n_dec, dec_len=dec_len)
    tn = _pick(D_MODEL, 1024)
    x4 = _mm_call(_mm_res_body, T, D_MODEL, tm_mid, tn,
                  [("lhs", o), ("w", w_o[0], 0), ("row", x3, True)], F32, 48, "o_proj")

    tm_moe = _pick(T, 512)
    n_tiles = (2 * T) // tm_moe + N_EXPERTS
    idx, gate_w = _router(x4, norm_ffn[1], moe_router[0])
    pos_sorted, src, tile_expert, tile_rows, tile_first = _route_plan(idx, tm_moe, n_tiles)
    xs = _moe_gather(x4, norm_ffn[1], src, tm_moe)
    ys = _moe_experts(xs, moe_w_gate_up[0], moe_w_down[0], tile_expert, tile_rows, tile_first, tm_moe)
    x5, h = _moe_combine(x4, ys, pos_sorted, gate_w, norm_ple[1])
    tn = _pick(D_MODEL, 512)
    x6 = _mm_call(functools.partial(_mm_ple_body, normed=False), T, D_MODEL, tm_wide, tn,
                  [("lhs", h), ("w", ple_gate[1], 0), ("lhs", p_all[1]), ("w", ple_proj[1], 0), ("row", x5, True)],
                  F32, 48, "ple1")
    y_prompt, y_sample = _rmsnorm_split(x6, norm_final, Tp)

    k_all = kv[:, :KW]
    v_all = kv[:, KW:]
    tail = lambda a: a[:Tp].reshape(n_prompt, seq, N_KV_HEADS, HEAD_DIM)[:, seq - WINDOW:]
    new = lambda a: a[Tp:].reshape(n_dec, dec_len, N_KV_HEADS, HEAD_DIM)
    k_sample = jnp.concatenate([cache_k, new(k_all)], axis=1)[:, -WINDOW:]
    v_sample = jnp.concatenate([cache_v, new(v_all)], axis=1)[:, -WINDOW:]
    return (y_prompt.reshape(n_prompt, seq, D_MODEL), y_sample.reshape(n_dec, dec_len, D_MODEL),
            conv_prompt[None], ssm_prompt[None], tail(k_all), tail(v_all),
            conv_sample[None], ssm_sample[None], k_sample, v_sample)
```

```python
import functools

import jax
import jax.numpy as jnp
from jax import lax
from jax.experimental import pallas as pl
from jax.experimental.pallas import tpu as pltpu

F32 = jnp.float32
BF16 = jnp.bfloat16
I32 = jnp.int32
HI = lax.Precision.HIGHEST

D_MODEL = 2048
D_INNER = 4096
SSM_HEADS = 64
SSM_HEAD_DIM = 64
SSM_GROUPS = 8
SSM_STATE = 128
CONV_WIDTH = 4
CONV_DIM = D_INNER + 2 * SSM_GROUPS * SSM_STATE
N_Q_HEADS = 32
N_KV_HEADS = 4
Q_PER_KV = N_Q_HEADS // N_KV_HEADS
HEAD_DIM = 64
WINDOW = 128
CHUNK = 64
ROPE_THETA = 10000.0
D_FF = 5632
N_EXPERTS = 8
EXPERT_FF = 7168
PLE_DIM = 256
PAST_LEN = 4096
EPS = 1e-6

LANES = 128
MIB = 1024 * 1024
NT_DIMS = (((1,), (1,)), ((), ()))
DMA_UNROLL = 8
CONV_COLS = 512
PAIR_W = 2 * SSM_HEAD_DIM
MOE_ROW_SPLIT = 4


def _params(n_axes, vmem_mib):
    return pltpu.CompilerParams(dimension_semantics=("arbitrary",) * n_axes,
                                vmem_limit_bytes=vmem_mib * MIB)


def _pick(n, pref):
    for t in (1536, 1024, 768, 512, 384, 256, 128, 96, 64, 32, 16, 8):
        if t <= pref and n % t == 0:
            return t
    raise ValueError(f"no tile for {n}")


def _dot(a, b):
    return jnp.dot(a, b, preferred_element_type=F32)


def _dot_nt(a, b, precision=None):
    return lax.dot_general(a, b, NT_DIMS, precision=precision, preferred_element_type=F32)


def _sigmoid(x):
    return 1.0 / (1.0 + jnp.exp(-x))


def _softplus(x):
    return jnp.maximum(x, 0.0) + jnp.log1p(jnp.exp(-jnp.abs(x)))


def _eye(n, dtype):
    r = lax.broadcasted_iota(I32, (n, n), 0)
    c = lax.broadcasted_iota(I32, (n, n), 1)
    return (r == c).astype(dtype)


def _rms_rows(x, g):
    inv = lax.rsqrt(jnp.mean(x * x, axis=-1, keepdims=True) + EPS)
    return x * inv * g


def _rms_body(x_ref, g_ref, o_ref):
    o_ref[...] = _rms_rows(x_ref[...], g_ref[...]).astype(o_ref.dtype)


def _rmsnorm(x, g, out_dtype):
    T, D = x.shape
    tr = _pick(T, 512)
    return pl.pallas_call(
        _rms_body, out_shape=jax.ShapeDtypeStruct((T, D), out_dtype), grid=(T // tr,),
        in_specs=[pl.BlockSpec((tr, D), lambda i: (i, 0)), pl.BlockSpec((1, D), lambda i: (0, 0))],
        out_specs=pl.BlockSpec((tr, D), lambda i: (i, 0)),
        compiler_params=_params(1, 32), name="rmsnorm")(x, g.reshape(1, D))


def _rms_split_body(x_ref, g_ref, o1_ref, o2_ref, *, n_first):
    i = pl.program_id(0)
    y = _rms_rows(x_ref[...], g_ref[...])

    @pl.when(i < n_first)
    def _():
        o1_ref[...] = y

    @pl.when(i >= n_first)
    def _():
        o2_ref[...] = y


def _rmsnorm_split(x, g, rows_first):
    T, D = x.shape
    rest = T - rows_first
    tr = _pick(rows_first, 512)
    while rest % tr:
        tr //= 2
    n_first = rows_first // tr
    return pl.pallas_call(
        functools.partial(_rms_split_body, n_first=n_first),
        out_shape=(jax.ShapeDtypeStruct((rows_first, D), F32), jax.ShapeDtypeStruct((rest, D), F32)), grid=(T // tr,),
        in_specs=[pl.BlockSpec((tr, D), lambda i: (i, 0)), pl.BlockSpec((1, D), lambda i: (0, 0))],
        out_specs=(pl.BlockSpec((tr, D), lambda i: (jnp.minimum(i, n_first - 1), 0)),
                   pl.BlockSpec((tr, D), lambda i: (jnp.maximum(i - n_first, 0), 0))),
        compiler_params=_params(1, 32), name="rmsnorm_final")(x, g.reshape(1, D))


def _mm_plain_body(x_ref, w_ref, o_ref):
    o_ref[...] = _dot(x_ref[...], w_ref[...]).astype(o_ref.dtype)


def _inv_rms(x_ref):
    x = x_ref[...].astype(F32)
    return lax.rsqrt(jnp.mean(x * x, axis=-1, keepdims=True) + EPS)


def _store_with_copy(y, o_ref, copy_ref):
    o_ref[...] = y
    if copy_ref is not None:
        copy_ref[...] = y.astype(copy_ref.dtype)


def _mm_res_body(x_ref, w_ref, r_ref, o_ref, copy_ref=None):
    _store_with_copy(r_ref[...] + _dot(x_ref[...], w_ref[...]), o_ref, copy_ref)


def _mm_swiglu_body(x_ref, wg_ref, wu_ref, o_ref, *, normed):
    x = x_ref[...]
    g = _dot(x, wg_ref[...])
    u = _dot(x, wu_ref[...])
    if normed:
        inv = _inv_rms(x_ref)
        g, u = g * inv, u * inv
    o_ref[...] = (g * _sigmoid(g) * u).astype(o_ref.dtype)


def _mm_ple_body(h_ref, wg_ref, p_ref, wp_ref, r_ref, o_ref, copy_ref=None, *, normed):
    a = _dot(h_ref[...], wg_ref[...])
    if normed:
        a = a * _inv_rms(h_ref)
    _store_with_copy(r_ref[...] + _sigmoid(a) * _dot(p_ref[...], wp_ref[...]), o_ref, copy_ref)


def _mm_rope_body(x_ref, w_ref, cos_ref, sin_ref, o_ref, *, n_rope, scale, normed):
    acc = _dot(x_ref[...], w_ref[...])
    if normed:
        acc = acc * _inv_rms(x_ref)
    tm, tn = acc.shape
    cos = cos_ref[...]
    sin = sin_ref[...]
    lane = lax.broadcasted_iota(I32, (tm, LANES), 1)
    first_half = (lane % HEAD_DIM) < (HEAD_DIM // 2)
    for c in range(tn // LANES):
        x = acc[:, c * LANES:(c + 1) * LANES]
        if c < n_rope:
            partner = jnp.where(first_half, pltpu.roll(x, LANES - HEAD_DIM // 2, 1),
                                pltpu.roll(x, HEAD_DIM // 2, 1))
            x = (x * cos + partner * sin) * scale
        o_ref[:, c * LANES:(c + 1) * LANES] = x.astype(o_ref.dtype)


def _with_weight_cast(body, w_slots, n_ops, n_in, n_out):
    def wrapped(*refs):
        ins, outs, scratch = list(refs[:n_ops]), refs[n_in:n_in + n_out], refs[n_in + n_out:]

        @pl.when(pl.program_id(1) == 0)
        def _():
            for (k, gk), s in zip(w_slots, scratch):
                w = ins[k][...]
                s[...] = (w if gk is None else w * refs[gk][...]).astype(BF16)

        for (k, _), s in zip(w_slots, scratch):
            ins[k] = s
        body(*ins, *outs)
    return wrapped


def _mm_call(body, T, N, tm, tn, operands, out_dtype, vmem_mib, name, copy_dtype=None):
    in_specs, args, w_slots, scratch, gains = [], [], [], [], []
    for op in operands:
        kind, a = op[0], op[1]
        if kind == "lhs":
            in_specs.append(pl.BlockSpec((tm, a.shape[1]), lambda j, i: (i, 0)))
        elif kind == "w":
            in_specs.append(pl.BlockSpec((a.shape[0], tn), functools.partial(lambda j, i, off: (0, j + off), off=op[2])))
            w_slots.append([len(args), None])
            if len(op) > 3:
                w_slots[-1][1] = len(operands) + len(gains)
                gains.append(op[3].reshape(-1, 1))
            scratch.append(pltpu.VMEM((a.shape[0], tn), BF16))
        elif op[2]:
            in_specs.append(pl.BlockSpec((tm, tn), lambda j, i: (i, j)))
        else:
            in_specs.append(pl.BlockSpec((tm, a.shape[1]), lambda j, i: (i, 0)))
        args.append(a)
    for g in gains:
        in_specs.append(pl.BlockSpec(g.shape, lambda j, i: (0, 0)))
        args.append(g)
    tile = pl.BlockSpec((tm, tn), lambda j, i: (i, j))
    out_shape = jax.ShapeDtypeStruct((T, N), out_dtype)
    n_out = 1
    if copy_dtype is not None:
        out_shape, tile, n_out = (out_shape, jax.ShapeDtypeStruct((T, N), copy_dtype)), (tile, tile), 2
    return pl.pallas_call(
        _with_weight_cast(body, w_slots, len(operands), len(args), n_out), out_shape=out_shape,
        grid=(N // tn, T // tm), in_specs=in_specs, out_specs=tile,
        scratch_shapes=scratch, compiler_params=_params(2, vmem_mib), name=name)(*args)


def _ssd_body(*refs, Q, nc, has_init):
    if has_init:
        (z_ref, xbc_ref, dt_ref, cw_ref, cb_ref, dtb_ref, alog_ref, dfull_ref, nw_ref, conv0_ref, ssm0_ref,
         _, g_ref, convo_ref, ssmo_ref, xp, HT, ysc, act, xsb) = refs
    else:
        (z_ref, xbc_ref, dt_ref, cw_ref, cb_ref, dtb_ref, alog_ref, dfull_ref, nw_ref,
         g_ref, convo_ref, ssmo_ref, xp, HT, ysc, act, xsb) = refs
    c = pl.program_id(1)
    HPG = SSM_HEADS // SSM_GROUPS
    eye_n_bf = _eye(SSM_STATE, BF16)

    @pl.when(c == 0)
    def _init():
        xp[0:8, :] = jnp.zeros((8, CONV_DIM), F32)
        if has_init:
            xp[8 - (CONV_WIDTH - 1):8, :] = conv0_ref[...]
            for g in range(SSM_GROUPS):
                HT[g] = ssm0_ref[g * HPG:(g + 1) * HPG].reshape(HPG * SSM_HEAD_DIM, SSM_STATE).T
        else:
            HT[...] = jnp.zeros(HT.shape, F32)

    xp[8:8 + Q, :] = xbc_ref[...]
    for k in range(CONV_DIM // CONV_COLS):
        cs = slice(k * CONV_COLS, (k + 1) * CONV_COLS)
        xe = xp[:, cs]
        cw = cw_ref[:, cs]
        conv = cb_ref[:, cs]
        for t in range(CONV_WIDTH - 1):
            conv = conv + pltpu.roll(xe, CONV_WIDTH - 1 - t, 0)[8:, :] * cw[t:t + 1, :]
        conv = conv + xe[8:, :] * cw[CONV_WIDTH - 1:CONV_WIDTH, :]
        a = conv * _sigmoid(conv)
        act[:, cs] = a
        if k * CONV_COLS < D_INNER:
            xsb[:, cs] = a.astype(BF16)

    @pl.when(c == nc - 1)
    def _conv_out():
        convo_ref[...] = xp[8 + Q - (CONV_WIDTH - 1):8 + Q, :]

    xp[0:8, :] = xp[Q:Q + 8, :]

    dt = _softplus(dt_ref[...] + dtb_ref[...])
    dA = dt * (-jnp.exp(alog_ref[...]))
    tril = lax.broadcasted_iota(I32, (Q, Q), 0) >= lax.broadcasted_iota(I32, (Q, Q), 1)
    a_cs = jnp.dot(tril.astype(F32), dA, precision=HI, preferred_element_type=F32)
    a_last = a_cs[Q - 1:Q, :]
    w_end = jnp.exp(a_last - a_cs) * dt
    cdec = jnp.exp(a_last)

    n_pairs = SSM_HEADS // 2
    pr = lax.broadcasted_iota(I32, (n_pairs, SSM_HEADS), 0)
    pc_ = lax.broadcasted_iota(I32, (n_pairs, SSM_HEADS), 1)
    sel_a = (pc_ == 2 * pr).astype(F32)
    sel_b = (pc_ == 2 * pr + 1).astype(F32)

    def pair_rows(m):
        return jnp.concatenate([_dot_nt(sel_a, m, HI), _dot_nt(sel_b, m, HI)], axis=1)

    a_csT2, dtT2, w_endT2 = pair_rows(a_cs), pair_rows(dt), pair_rows(w_end)
    tok_r =lax.broadcasted_iota(I32, (Q, 2 * Q), 0)
    tok_c = lax.broadcasted_iota(I32, (Q, 2 * Q), 1)
    first_tok = tok_c < Q
    tril2 = tok_r >= jnp.where(first_tok, tok_c, tok_c - Q)
    first_ch = lax.broadcasted_iota(I32, (Q, PAIR_W), 1) < SSM_HEAD_DIM
    first_ch2 = lax.broadcasted_iota(I32, (2 * Q, PAIR_W), 1) < SSM_HEAD_DIM
    first_row = lax.broadcasted_iota(I32, (2 * Q, PAIR_W), 0) < Q
    ssq = jnp.zeros((Q, PAIR_W), F32)

    for g in range(SSM_GROUPS):
        b0 = D_INNER + g * SSM_STATE
        c0 = D_INNER + SSM_GROUPS * SSM_STATE + g * SSM_STATE
        Bg = act[:, b0:b0 + SSM_STATE].astype(BF16)
        Cg = act[:, c0:c0 + SSM_STATE].astype(BF16)
        cb = _dot_nt(Cg, Bg)
        cb2 = jnp.concatenate([cb, cb], axis=1)
        BT = _dot_nt(eye_n_bf, Bg)
        BT2 = jnp.concatenate([BT, BT], axis=1)
        Hg = HT[g]
        yoff = _dot(Cg, Hg.astype(BF16))
        for pp in range(HPG // 2):
            i = g * (HPG // 2) + pp
            ha, hb = 2 * i, 2 * i + 1
            pc = slice(i * PAIR_W, (i + 1) * PAIR_W)
            gc = slice(pp * PAIR_W, (pp + 1) * PAIR_W)
            col_a, col_b = a_cs[:, ha:ha + 1], a_cs[:, hb:hb + 1]
            seg = jnp.where(first_tok, col_a, col_b) - a_csT2[i:i + 1, :]
            dec = jnp.where(tril2, jnp.exp(seg), 0.0)
            M2 = (cb2 * dec * dtT2[i:i + 1, :]).astype(BF16)
            BTw2 = (BT2 * w_endT2[i:i + 1, :]).astype(BF16)
            x2 = xsb[:, pc]
            x2 = jnp.concatenate([x2, x2], axis=0)
            xbd = jnp.where(first_row == first_ch2, x2, jnp.zeros_like(x2))
            r = _dot(jnp.concatenate([M2, BTw2], axis=0), xbd)
            ea2 = jnp.exp(jnp.where(first_ch, col_a, col_b))
            y = r[:Q] + yoff[:, gc] * ea2 + act[:, pc] * dfull_ref[:, pc]
            zf = z_ref[:, pc].astype(F32)
            y = y * (zf * _sigmoid(zf))
            ysc[:, pc] = y
            ssq = ssq + y * y
            cd2 = jnp.where(first_ch[0:1, :], cdec[:, ha:ha + 1], cdec[:, hb:hb + 1])
            HT[g, :, gc] = Hg[:, gc] * cd2 + r[Q:]

    inv = lax.rsqrt(jnp.sum(ssq, axis=-1, keepdims=True) * (1.0 / D_INNER) + EPS)
    g_ref[...] = (ysc[...] * inv * nw_ref[...]).astype(g_ref.dtype)

    @pl.when(c == nc - 1)
    def _state_out():
        for g in range(SSM_GROUPS):
            ssmo_ref[g * HPG:(g + 1) * HPG] = HT[g].T.reshape(HPG, SSM_HEAD_DIM, SSM_STATE)


def _ssd_call(z, xbc, dt, prm, *, n_seq, seq_len, row0, Q, conv0=None, ssm0=None, g_in=None):
    T = z.shape[0]
    nc = seq_len // Q
    blk0 = row0 // Q
    has_init = conv0 is not None
    rows = lambda b, c: (blk0 + b * nc + c, 0)
    const = lambda b, c: (0, 0)
    in_specs = [pl.BlockSpec((Q, D_INNER), rows), pl.BlockSpec((Q, CONV_DIM), rows), pl.BlockSpec((Q, SSM_HEADS), rows),
                pl.BlockSpec((CONV_WIDTH, CONV_DIM), const), pl.BlockSpec((1, CONV_DIM), const),
                pl.BlockSpec((1, SSM_HEADS), const), pl.BlockSpec((1, SSM_HEADS), const),
                pl.BlockSpec((1, D_INNER), const), pl.BlockSpec((1, D_INNER), const)]
    args = [z, xbc, dt, prm["conv_w"], prm["conv_b"], prm["dt_bias"], prm["a_log"], prm["d_full"], prm["norm_w"]]
    aliases = {}
    if has_init:
        in_specs += [pl.BlockSpec((None, CONV_WIDTH - 1, CONV_DIM), lambda b, c: (b, 0, 0)),
                     pl.BlockSpec((None, SSM_HEADS, SSM_HEAD_DIM, SSM_STATE), lambda b, c: (b, 0, 0, 0)),
                     pl.BlockSpec(memory_space=pl.ANY)]
        args += [conv0, ssm0, g_in]
        aliases = {len(args) - 1: 0}
    out_shape = (jax.ShapeDtypeStruct((T, D_INNER), BF16),
                 jax.ShapeDtypeStruct((n_seq, CONV_WIDTH - 1, CONV_DIM), F32),
                 jax.ShapeDtypeStruct((n_seq, SSM_HEADS, SSM_HEAD_DIM, SSM_STATE), F32))
    out_specs = (pl.BlockSpec((Q, D_INNER), rows),
                 pl.BlockSpec((None, CONV_WIDTH - 1, CONV_DIM), lambda b, c: (b, 0, 0)),
                 pl.BlockSpec((None, SSM_HEADS, SSM_HEAD_DIM, SSM_STATE), lambda b, c: (b, 0, 0, 0)))
    scratch = [pltpu.VMEM((Q + 8, CONV_DIM), F32),
               pltpu.VMEM((SSM_GROUPS, SSM_STATE, D_INNER // SSM_GROUPS), F32),
               pltpu.VMEM((Q, D_INNER), F32), pltpu.VMEM((Q, CONV_DIM), F32), pltpu.VMEM((Q, D_INNER), BF16)]
    return pl.pallas_call(
        functools.partial(_ssd_body, Q=Q, nc=nc, has_init=has_init), out_shape=out_shape,
        grid=(n_seq, nc), in_specs=in_specs, out_specs=out_specs, scratch_shapes=scratch,
        input_output_aliases=aliases, compiler_params=_params(2, 40),
        name="ssd_sample" if has_init else "ssd_prompt")(*args)


def _attend(q_ref, sink_ref, o_ref, kband, vband, bias, stack):
    Lq = q_ref.shape[0]
    ones = jnp.ones((kband.shape[1], HEAD_DIM), BF16)
    for h0 in range(0, N_Q_HEADS, stack):
        kh = h0 // Q_PER_KV
        heads = range(h0, h0 + stack)
        q = jnp.concatenate([q_ref[:, h * HEAD_DIM:(h + 1) * HEAD_DIM] for h in heads], axis=0)
        s = _dot_nt(q, kband[kh])
        if bias is not None:
            s = s + bias
        es, sink_es = [], []
        for k, h in enumerate(heads):
            sh = s[k * Lq:(k + 1) * Lq]
            sink = sink_ref[h]
            m = jnp.maximum(jnp.max(sh, axis=-1, keepdims=True), sink)
            es.append(jnp.exp(sh - m).astype(BF16))
            sink_es.append(jnp.exp(sink - m))
        e = jnp.concatenate(es, axis=0)
        den = _dot(e, ones)
        o = _dot(e, vband[kh])
        for k, h in enumerate(heads):
            rows = slice(k * Lq, (k + 1) * Lq)
            o_ref[:, h * HEAD_DIM:(h + 1) * HEAD_DIM] = (o[rows] / (den[rows] + sink_es[k])).astype(o_ref.dtype)


def _stage_kv(kband, vband, row0, k_of, v_of):
    n = k_of(0).shape[0]
    for kh in range(N_KV_HEADS):
        kband[kh, row0:row0 + n, :] = k_of(kh).astype(BF16)
        vband[kh, row0:row0 + n, :] = v_of(kh).astype(BF16)


def _attn_prompt_body(sink_ref, q_ref, kv0_ref, kv1_ref, kv2_ref, o_ref, kband, vband):
    c = pl.program_id(1)
    KW = N_KV_HEADS * HEAD_DIM
    for j, ref in enumerate((kv0_ref, kv1_ref, kv2_ref)):
        _stage_kv(kband, vband, j * CHUNK,
                  functools.partial(lambda kh, r: r[:, kh * HEAD_DIM:(kh + 1) * HEAD_DIM], r=ref),
                  functools.partial(lambda kh, r: r[:, KW + kh * HEAD_DIM:KW + (kh + 1) * HEAD_DIM], r=ref))
    key = lax.broadcasted_iota(I32, (1, kband.shape[1]), 1)
    bias = jnp.where(key >= jnp.maximum(2 - c, 0) * CHUNK, 0.0, -jnp.inf).astype(F32)
    _attend(q_ref, sink_ref, o_ref, kband, vband, bias, 1)


def _attn_sample_body(sink_ref, q_ref, ck_ref, cv_ref, kvn_ref, _, o_ref, kband, vband):
    KW = N_KV_HEADS * HEAD_DIM
    _stage_kv(kband, vband, 0, lambda kh: ck_ref[:, kh * HEAD_DIM:(kh + 1) * HEAD_DIM],
              lambda kh: cv_ref[:, kh * HEAD_DIM:(kh + 1) * HEAD_DIM])
    _stage_kv(kband, vband, WINDOW, lambda kh: kvn_ref[:, kh * HEAD_DIM:(kh + 1) * HEAD_DIM],
              lambda kh: kvn_ref[:, KW + kh * HEAD_DIM:KW + (kh + 1) * HEAD_DIM])
    _attend(q_ref, sink_ref, o_ref, kband, vband, None, max(1, min(Q_PER_KV, CHUNK // q_ref.shape[0])))


def _attention(q, kv, sinks, cache_k, cache_v, *, n_prompt, seq, n_dec, dec_len):
    T = q.shape[0]
    NQ = N_Q_HEADS * HEAD_DIM
    KW = N_KV_HEADS * HEAD_DIM
    nc = seq // CHUNK
    smem = pl.BlockSpec(memory_space=pltpu.SMEM)
    band = lambda back: (lambda b, c: (b * nc + jnp.maximum(c - back, 0), 0))
    bands = lambda keys: [pltpu.VMEM((N_KV_HEADS, keys, HEAD_DIM), BF16)] * 2
    o = pl.pallas_call(
        _attn_prompt_body, out_shape=jax.ShapeDtypeStruct((T, NQ), BF16), grid=(n_prompt, nc),
        in_specs=[smem, pl.BlockSpec((CHUNK, NQ), lambda b, c: (b * nc + c, 0)),
                  pl.BlockSpec((CHUNK, 2 * KW), band(2)), pl.BlockSpec((CHUNK, 2 * KW), band(1)),
                  pl.BlockSpec((CHUNK, 2 * KW), band(0))],
        out_specs=pl.BlockSpec((CHUNK, NQ), lambda b, c: (b * nc + c, 0)),
        scratch_shapes=bands(WINDOW + CHUNK),
        compiler_params=_params(2, 32), name="attn_prompt")(sinks, q, kv, kv, kv)
    blk0 = (n_prompt * seq) // dec_len
    return pl.pallas_call(
        _attn_sample_body, out_shape=jax.ShapeDtypeStruct((T, NQ), BF16), grid=(n_dec,),
        in_specs=[smem, pl.BlockSpec((dec_len, NQ), lambda b: (blk0 + b, 0)),
                  pl.BlockSpec((None, WINDOW, KW), lambda b: (b, 0, 0)),
                  pl.BlockSpec((None, WINDOW, KW), lambda b: (b, 0, 0)),
                  pl.BlockSpec((dec_len, 2 * KW), lambda b: (blk0 + b, 0)),
                  pl.BlockSpec(memory_space=pl.ANY)],
        out_specs=pl.BlockSpec((dec_len, NQ), lambda b: (blk0 + b, 0)),
        scratch_shapes=bands(WINDOW + dec_len),
        input_output_aliases={5: 0}, compiler_params=_params(1, 32), name="attn_sample")(
            sinks, q, cache_k.reshape(n_dec, WINDOW, KW), cache_v.reshape(n_dec, WINDOW, KW), kv, o)


def _router_body(x_ref, g_ref, rt_ref, idx_ref, w_ref):
    h = _rms_rows(x_ref[...], g_ref[...])
    lt = _dot_nt(rt_ref[...], h, HI)
    ids = lax.broadcasted_iota(I32, lt.shape, 0)
    m1 = jnp.max(lt, axis=0, keepdims=True)
    i1 = jnp.min(jnp.where(lt == m1, ids, N_EXPERTS), axis=0, keepdims=True)
    rest = jnp.where(ids == i1, -jnp.inf, lt)
    m2 = jnp.max(rest, axis=0, keepdims=True)
    i2 = jnp.min(jnp.where(rest == m2, ids, N_EXPERTS), axis=0, keepdims=True)
    e2 = jnp.exp(m2 - m1)
    w1 = 1.0 / (1.0 + e2)
    idx_ref[...] = jnp.concatenate([i1, i2], axis=0)
    w_ref[...] = jnp.concatenate([w1, e2 * w1], axis=0)


def _router(x, g, router):
    T = x.shape[0]
    tm = _pick(T, 512)
    return pl.pallas_call(
        _router_body, out_shape=(jax.ShapeDtypeStruct((2, T), I32), jax.ShapeDtypeStruct((2, T), F32)),
        grid=(T // tm,),
        in_specs=[pl.BlockSpec((tm, D_MODEL), lambda i: (i, 0)), pl.BlockSpec((1, D_MODEL), lambda i: (0, 0)),
                  pl.BlockSpec((N_EXPERTS, D_MODEL), lambda i: (0, 0))],
        out_specs=(pl.BlockSpec((2, tm), lambda i: (0, i)), pl.BlockSpec((2, tm), lambda i: (0, i))),
        compiler_params=_params(1, 32), name="moe_router")(x, g.reshape(1, D_MODEL), router.T)


def _route_plan(idx, tm, n_tiles):
    T = idx.shape[1]
    e_flat = idx.reshape(-1)
    onehot = (e_flat[:, None] == jnp.arange(N_EXPERTS, dtype=I32)[None, :]).astype(I32)
    csum = jnp.cumsum(onehot, axis=0)
    counts = csum[-1]
    rank = jnp.sum(csum * onehot, axis=1) - 1
    padded = ((counts + tm - 1) // tm) * tm
    ends = jnp.cumsum(padded)
    pos = jnp.sum((ends - padded)[None, :] * onehot, axis=1) + rank
    tok = jnp.tile(jnp.arange(T, dtype=I32), 2)
    src = jnp.zeros((n_tiles * tm,), I32).at[pos].set(tok)
    tile_start = jnp.arange(n_tiles, dtype=I32) * tm
    valid = tile_start < ends[-1]
    probe = jnp.minimum(tile_start, ends[-1] - 1)
    expert = jnp.minimum(jnp.sum((ends[None, :] <= probe[:, None]).astype(I32), axis=1), N_EXPERTS - 1)
    first = jnp.concatenate([jnp.ones((1,), I32), (expert[1:] != expert[:-1]).astype(I32)])
    real_end = jnp.sum((ends - padded + counts)[None, :] * (expert[:, None] == jnp.arange(N_EXPERTS, dtype=I32)[None, :]), axis=1)
    step = tm // MOE_ROW_SPLIT
    rows = jnp.clip(real_end - tile_start, 0, tm)
    rows = jnp.where(valid, ((rows + step - 1) // step) * step, 0)
    return pos.reshape(2, T).astype(I32), src, expert, rows.astype(I32), first


def _row_copy(table_hbm, row, buf, r, sem):
    return pltpu.make_async_copy(table_hbm.at[pl.ds(row, 1)], buf.at[pl.ds(r, 1)], sem)


def _start_rows(index_of, n_rows, table_hbm, buf, sem):
    def start(r, carry):
        _row_copy(table_hbm, index_of(r), buf, r, sem).start()
        return carry
    lax.fori_loop(0, n_rows, start, 0, unroll=DMA_UNROLL)


def _wait_rows(n_rows, table_hbm, buf, sem):
    def wait(r, carry):
        _row_copy(table_hbm, 0, buf, r, sem).wait()
        return carry
    lax.fori_loop(0, n_rows, wait, 0, unroll=DMA_UNROLL)


def _moe_gather_body(src_ref, nxt_ref, x_hbm, g_ref, o_ref, buf, sem):
    i = pl.program_id(0)
    tm = buf.shape[1]
    slot = i % 2

    @pl.when(i == 0)
    def _():
        _start_rows(lambda r: src_ref[0, 0, r], tm, x_hbm, buf.at[0], sem.at[0])

    @pl.when(i + 1 < pl.num_programs(0))
    def _():
        _start_rows(lambda r: nxt_ref[0, 0, r], tm, x_hbm, buf.at[1 - slot], sem.at[1 - slot])

    _wait_rows(tm, x_hbm, buf.at[slot], sem.at[slot])
    o_ref[...] = _rms_rows(buf[slot], g_ref[...]).astype(o_ref.dtype)


def _moe_gather(x, g, src, tm):
    n_tiles = src.shape[0] // tm
    src3 = src.reshape(n_tiles, 1, tm)
    return pl.pallas_call(
        _moe_gather_body, out_shape=jax.ShapeDtypeStruct((n_tiles * tm, D_MODEL), BF16), grid=(n_tiles,),
        in_specs=[pl.BlockSpec((1, 1, tm), lambda i: (i, 0, 0), memory_space=pltpu.SMEM),
                  pl.BlockSpec((1, 1, tm), lambda i: (jnp.minimum(i + 1, n_tiles - 1), 0, 0), memory_space=pltpu.SMEM),
                  pl.BlockSpec(memory_space=pl.ANY), pl.BlockSpec((1, D_MODEL), lambda i: (0, 0))],
        out_specs=pl.BlockSpec((tm, D_MODEL), lambda i: (i, 0)),
        scratch_shapes=[pltpu.VMEM((2, tm, D_MODEL), F32), pltpu.SemaphoreType.DMA((2,))],
        compiler_params=_params(1, 32), name="moe_gather")(src3, src3, x, g.reshape(1, D_MODEL))


def _for_leading_rows(rows, o_ref, compute):
    tm = o_ref.shape[0]
    step = tm // MOE_ROW_SPLIT
    for n in range(0, tm + 1, step):
        @pl.when(rows == n)
        def _(n=n):
            if n:
                o_ref[0:n, :] = compute(n).astype(o_ref.dtype)
            if n < tm:
                o_ref[n:tm, :] = jnp.zeros((tm - n, o_ref.shape[1]), o_ref.dtype)


def _moe_up_body(te_ref, tr_ref, tf_ref, x_ref, wg_ref, wu_ref, wd_ref, o_ref, wdo_ref, wg_bf, wu_bf):
    i = pl.program_id(1)
    wdo_ref[...] = wd_ref[...].astype(BF16)

    @pl.when(tf_ref[i] != 0)
    def _():
        wg_bf[...] = wg_ref[...].astype(BF16)
        wu_bf[...] = wu_ref[...].astype(BF16)

    def swiglu(n):
        x = x_ref[0:n, :]
        g = _dot(x, wg_bf[...])
        return g * _sigmoid(g) * _dot(x, wu_bf[...])

    _for_leading_rows(tr_ref[i], o_ref, swiglu)


def _moe_down_body(te_ref, tr_ref, x_ref, w_ref, o_ref):
    i = pl.program_id(1)
    _for_leading_rows(tr_ref[i], o_ref, lambda n: _dot(x_ref[0:n, :], w_ref[...]))


def _moe_experts(xs, w_gate_up, w_down, tile_expert, tile_rows, tile_first, tm):
    P = xs.shape[0]
    n_tiles = P // tm
    tn = _pick(EXPERT_FF, 1024)
    up_blocks = EXPERT_FF // tn
    steps = up_blocks * n_tiles
    wd_rows = N_EXPERTS * EXPERT_FF
    cast_rows = min(d for d in range(16, wd_rows + 1, 16) if wd_rows % d == 0 and d * steps >= wd_rows)
    cast_block = lambda j, i, te, tv, tf: (jnp.minimum(j * n_tiles + i, wd_rows // cast_rows - 1), 0)
    act, w_down = pl.pallas_call(
        _moe_up_body,
        out_shape=(jax.ShapeDtypeStruct((P, EXPERT_FF), BF16), jax.ShapeDtypeStruct((wd_rows, D_MODEL), BF16)),
        grid_spec=pltpu.PrefetchScalarGridSpec(
            num_scalar_prefetch=3, grid=(up_blocks, n_tiles),
            in_specs=[pl.BlockSpec((tm, D_MODEL), lambda j, i, te, tv, tf: (i, 0)),
                      pl.BlockSpec((None, D_MODEL, tn), lambda j, i, te, tv, tf: (te[i], 0, j)),
                      pl.BlockSpec((None, D_MODEL, tn), lambda j, i, te, tv, tf: (te[i], 0, j + up_blocks)),
                      pl.BlockSpec((cast_rows, D_MODEL), cast_block)],
            out_specs=(pl.BlockSpec((tm, tn), lambda j, i, te, tv, tf: (i, j)),
                       pl.BlockSpec((cast_rows, D_MODEL), cast_block)),
            scratch_shapes=[pltpu.VMEM((D_MODEL, tn), BF16)] * 2),
        compiler_params=_params(2, 60), name="moe_up")(
            tile_expert, tile_rows, tile_first, xs, w_gate_up, w_gate_up, w_down.reshape(wd_rows, D_MODEL))
    w_down = w_down.reshape(N_EXPERTS, EXPERT_FF, D_MODEL)
    tn = _pick(D_MODEL, 1024)
    return pl.pallas_call(
        _moe_down_body, out_shape=jax.ShapeDtypeStruct((P, D_MODEL), F32),
        grid_spec=pltpu.PrefetchScalarGridSpec(
            num_scalar_prefetch=2, grid=(D_MODEL // tn, n_tiles),
            in_specs=[pl.BlockSpec((tm, EXPERT_FF), lambda j, i, te, tv: (i, 0)),
                      pl.BlockSpec((None, EXPERT_FF, tn), lambda j, i, te, tv: (te[i], 0, j))],
            out_specs=pl.BlockSpec((tm, tn), lambda j, i, te, tv: (i, j))),
        compiler_params=_params(2, 56), name="moe_down")(tile_expert, tile_rows, act, w_down)


def _moe_combine_body(pos_ref, nxt_ref, x_ref, w_ref, ys_hbm, g_ref, xo_ref, ho_ref, buf, sem):
    i = pl.program_id(0)
    tc = buf.shape[2]
    slot = i % 2

    def start(table, s):
        for k in range(2):
            _start_rows(functools.partial(lambda r, k: table[0, k, r], k=k), tc, ys_hbm, buf.at[s, k], sem.at[s])

    @pl.when(i == 0)
    def _():
        start(pos_ref, 0)

    @pl.when(i + 1 < pl.num_programs(0))
    def _():
        start(nxt_ref, 1 - slot)

    for k in range(2):
        _wait_rows(tc, ys_hbm, buf.at[slot, k], sem.at[slot])
    w = w_ref[...]
    x = x_ref[...] + (w[:, 0:1] * buf[slot, 0] + w[:, 1:2] * buf[slot, 1])
    xo_ref[...] = x
    ho_ref[...] = _rms_rows(x, g_ref[...]).astype(ho_ref.dtype)


def _moe_combine(x, ys, pos, w, g):
    T = x.shape[0]
    tc = _pick(T, 256)
    n = T // tc
    pos_t = pos.reshape(2, n, tc).transpose(1, 0, 2)
    return pl.pallas_call(
        _moe_combine_body,
        out_shape=(jax.ShapeDtypeStruct((T, D_MODEL), F32), jax.ShapeDtypeStruct((T, D_MODEL), BF16)),
        grid=(n,),
        in_specs=[pl.BlockSpec((1, 2, tc), lambda i: (i, 0, 0), memory_space=pltpu.SMEM),
                  pl.BlockSpec((1, 2, tc), lambda i: (jnp.minimum(i + 1, n - 1), 0, 0), memory_space=pltpu.SMEM),
                  pl.BlockSpec((tc, D_MODEL), lambda i: (i, 0)), pl.BlockSpec((tc, 2), lambda i: (i, 0)),
                  pl.BlockSpec(memory_space=pl.ANY), pl.BlockSpec((1, D_MODEL), lambda i: (0, 0))],
        out_specs=(pl.BlockSpec((tc, D_MODEL), lambda i: (i, 0)), pl.BlockSpec((tc, D_MODEL), lambda i: (i, 0))),
        scratch_shapes=[pltpu.VMEM((2, 2, tc, D_MODEL), F32), pltpu.SemaphoreType.DMA((2,))],
        compiler_params=_params(1, 40), name="moe_combine")(pos_t, pos_t, x, w.T, ys, g.reshape(1, D_MODEL))


def _rope_tables(pos):
    half = HEAD_DIM // 2
    inv_freq = ROPE_THETA ** (-jnp.arange(half, dtype=F32) * 2.0 / HEAD_DIM)
    ang = pos.astype(F32)[:, None] * inv_freq[None, :]
    cos, sin = jnp.cos(ang), jnp.sin(ang)
    reps = LANES // HEAD_DIM
    return jnp.tile(jnp.concatenate([cos, cos], axis=-1), (1, reps)), jnp.tile(jnp.concatenate([-sin, sin], axis=-1), (1, reps))


def kernel(x_prompt, x_sample, p_prompt, p_sample, cache_conv, state_ssm, cache_k, cache_v, norm_mix, norm_ffn, norm_ple, norm_final, mamba_in_proj, mamba_conv_w, mamba_conv_b, mamba_dt_bias, mamba_a_log, mamba_d, mamba_norm, mamba_out_proj, norm_kv, w_kv, w_q, attn_sinks, w_o, ffn_w_gate_up, ffn_w_down, moe_router, moe_w_gate_up, moe_w_down, ple_proj, ple_gate):
    n_prompt, seq, _ = x_prompt.shape
    n_dec, dec_len, _ = x_sample.shape
    Tp = n_prompt * seq
    T = Tp + n_dec * dec_len
    assert seq % CHUNK == 0 and Tp % dec_len == 0 and dec_len % 8 == 0

    x0 = jnp.concatenate([x_prompt.reshape(Tp, D_MODEL), x_sample.reshape(-1, D_MODEL)], axis=0)
    p_all = jnp.concatenate([p_prompt.reshape(2, Tp, PLE_DIM), p_sample.reshape(2, -1, PLE_DIM)], axis=1).astype(BF16)
    pos = jnp.concatenate([jnp.tile(jnp.arange(seq, dtype=I32), n_prompt),
                           PAST_LEN + jnp.tile(jnp.arange(dec_len, dtype=I32), n_dec)])
    cos_t, sin_t = _rope_tables(pos)

    tm_wide = _pick(T, 1536)
    tm_mid = _pick(T, 768)
    tm_deep = _pick(T, 512)

    w_in = mamba_in_proj[0]
    h = _rmsnorm(x0, norm_mix[0], BF16)
    tn = _pick(D_INNER, 1024)
    z = _mm_call(_mm_plain_body, T, D_INNER, tm_wide, tn, [("lhs", h), ("w", w_in, 0)], BF16, 56, "in_proj_z")
    xbc = _mm_call(_mm_plain_body, T, CONV_DIM, tm_wide, tn, [("lhs", h), ("w", w_in, D_INNER // tn)], F32, 56, "in_proj_xbc")
    dt = _mm_call(_mm_plain_body, T, SSM_HEADS, tm_wide, SSM_HEADS,
                  [("lhs", h), ("w", w_in[:, D_INNER + CONV_DIM:], 0)], F32, 32, "in_proj_dt")
    ssd_prm = dict(conv_w=mamba_conv_w[0], conv_b=mamba_conv_b[0].reshape(1, CONV_DIM),
                   dt_bias=mamba_dt_bias[0].reshape(1, SSM_HEADS), a_log=mamba_a_log[0].reshape(1, SSM_HEADS),
                   d_full=jnp.repeat(mamba_d[0], SSM_HEAD_DIM).reshape(1, D_INNER),
                   norm_w=mamba_norm[0].reshape(1, D_INNER))
    g, conv_prompt, ssm_prompt = _ssd_call(z, xbc, dt, ssd_prm, n_seq=n_prompt, seq_len=seq, row0=0, Q=CHUNK)
    g, conv_sample, ssm_sample = _ssd_call(z, xbc, dt, ssd_prm, n_seq=n_dec, seq_len=dec_len, row0=Tp, Q=dec_len,
                                           conv0=cache_conv[0], ssm0=state_ssm[0], g_in=g)
    tn = _pick(D_MODEL, 512)
    x1, x1b = _mm_call(_mm_res_body, T, D_MODEL, tm_mid, tn,
                       [("lhs", g), ("w", mamba_out_proj[0], 0), ("row", x0, True)], F32, 48, "out_proj", copy_dtype=BF16)

    tn = _pick(D_FF, 512)
    w_gu = ffn_w_gate_up[0]
    a = _mm_call(functools.partial(_mm_swiglu_body, normed=True), T, D_FF, tm_wide, tn,
                 [("lhs", x1b), ("w", w_gu, 0, norm_ffn[0]), ("w", w_gu, D_FF // tn, norm_ffn[0])], BF16, 56, "ffn_up")
    tn = _pick(D_MODEL, 512)
    x2, x2b = _mm_call(_mm_res_body, T, D_MODEL, tm_deep, tn,
                       [("lhs", a), ("w", ffn_w_down[0], 0), ("row", x1, True)], F32, 52, "ffn_down", copy_dtype=BF16)
    tn = _pick(D_MODEL, 512)
    x3, x3b = _mm_call(functools.partial(_mm_ple_body, normed=True), T, D_MODEL, tm_wide, tn,
                       [("lhs", x2b), ("w", ple_gate[0], 0, norm_ple[0]), ("lhs", p_all[0]), ("w", ple_proj[0], 0),
                        ("row", x2, True)], F32, 48, "ple0", copy_dtype=BF16)

    KW = N_KV_HEADS * HEAD_DIM
    kv = _mm_call(functools.partial(_mm_rope_body, n_rope=KW // LANES, scale=1.0, normed=True), T, 2 * KW, tm_wide, 2 * KW,
                  [("lhs", x3b), ("w", w_kv, 0, norm_kv), ("row", cos_t, False), ("row", sin_t, False)], F32, 48, "kv_proj")
    tn = _pick(D_MODEL, 1024)
    q = _mm_call(functools.partial(_mm_rope_body, n_rope=tn // LANES, scale=HEAD_DIM ** -0.5, normed=True),
                 T, D_MODEL, tm_wide, tn,
                 [("lhs", x3b), ("w", w_q[0], 0, norm_mix[1]), ("row", cos_t, False), ("row", sin_t, False)],
                 BF16, 56, "q_proj")
    o = _attention(q, kv, attn_sinks[0], cache_k, cache_v, n_prompt=n_prompt, seq=seq, n_dec=n_dec, dec_len=dec_len)
    tn = _pick(D_MODEL, 1024)
    x4 = _mm_call(_mm_res_body, T, D_MODEL, tm_mid, tn,
                  [("lhs", o), ("w", w_o[0], 0), ("row", x3, True)], F32, 48, "o_proj")

    tm_moe = _pick(T, 512)
    n_tiles = (2 * T) // tm_moe + N_EXPERTS
    idx, gate_w = _router(x4, norm_ffn[1], moe_router[0])
    pos_sorted, src, tile_expert, tile_rows, tile_first = _route_plan(idx, tm_moe, n_tiles)
    xs = _moe_gather(x4, norm_ffn[1], src, tm_moe)
    ys = _moe_experts(xs, moe_w_gate_up[0], moe_w_down[0], tile_expert, tile_rows, tile_first, tm_moe)
    x5, h = _moe_combine(x4, ys, pos_sorted, gate_w, norm_ple[1])
    tn = _pick(D_MODEL, 512)
    x6 = _mm_call(functools.partial(_mm_ple_body, normed=False), T, D_MODEL, tm_wide, tn,
                  [("lhs", h), ("w", ple_gate[1], 0), ("lhs", p_all[1]), ("w", ple_proj[1], 0), ("row", x5, True)],
                  F32, 48, "ple1")
    y_prompt, y_sample = _rmsnorm_split(x6, norm_final, Tp)

    k_all = kv[:, :KW]
    v_all = kv[:, KW:]
    tail = lambda a: a[:Tp].reshape(n_prompt, seq, N_KV_HEADS, HEAD_DIM)[:, seq - WINDOW:]
    new = lambda a: a[Tp:].reshape(n_dec, dec_len, N_KV_HEADS, HEAD_DIM)
    k_sample = jnp.concatenate([cache_k, new(k_all)], axis=1)[:, -WINDOW:]
    v_sample = jnp.concatenate([cache_v, new(v_all)], axis=1)[:, -WINDOW:]
    return (y_prompt.reshape(n_prompt, seq, D_MODEL), y_sample.reshape(n_dec, dec_len, D_MODEL),
            conv_prompt[None], ssm_prompt[None], tail(k_all), tail(v_all),
            conv_sample[None], ssm_sample[None], k_sample, v_sample)
```
